```python
import jax, jax.numpy as jnp
from jax import lax
import numpy as np

D_MODEL = 1024
BATCH = 2
SEQ = 16384
DEPTH = 2

RWKV_HEAD_DIM = 64
RWKV_WIDTH = D_MODEL // 2
RWKV_HEADS = RWKV_WIDTH // RWKV_HEAD_DIM
RWKV_DECAY_LORA = 64
RWKV_AAA_LORA = 64
RWKV_GATE_LORA = 128
RWKV_LN_EPS = 64e-5

HGRN_HEAD_DIM = 128
HGRN_WIDTH = D_MODEL // 2
HGRN_HEADS = HGRN_WIDTH // HGRN_HEAD_DIM
HGRN_VALUE_DIM = HGRN_WIDTH // HGRN_HEADS
HGRN_CHUNK = 64
F_FLOOR = 1e-30

MLA_HEADS = 8
MLA_Q_RANK = D_MODEL // 4
MLA_KV_RANK = D_MODEL // 8
MLA_NOPE_DIM = 64
MLA_ROPE_DIM = 32
MLA_V_DIM = 64
MLA_WIDTH = MLA_HEADS * MLA_V_DIM
MLA_SCALE = (MLA_NOPE_DIM + MLA_ROPE_DIM) ** -0.5
Q_BLOCK = 128
ROPE_THETA = 10000.0
NEG_BIG = -1e30

N_BRANCHES = 3
D_FF = 2816
CONV_WIDTH = 3
NORM_EPS = 1e-6

RW_SIZES = (RWKV_WIDTH, RWKV_WIDTH, RWKV_WIDTH, RWKV_DECAY_LORA, RWKV_AAA_LORA, RWKV_GATE_LORA)
HG_SIZES = (HGRN_WIDTH, HGRN_WIDTH, HGRN_WIDTH, HGRN_WIDTH)
MLA_SIZES = (MLA_Q_RANK, MLA_KV_RANK, MLA_ROPE_DIM)
RW_COLS = sum(RW_SIZES)
HG_COLS = sum(HG_SIZES)
MLA_COLS = sum(MLA_SIZES)
GATE_COLS = N_BRANCHES * D_MODEL
IN_SIZES = (RW_COLS, HG_COLS, MLA_COLS, GATE_COLS)
IN_WIDTH = sum(IN_SIZES)

kernel_name = 'hybrid_rwkv7_hgrn2_mla_adaln_trunk'


def _split(u, sizes):
    cuts, acc = [], 0
    for s in sizes[:-1]:
        acc += s
        cuts.append(acc)
    return jnp.split(u, cuts, axis=-1)


def _rmsnorm(x, eps=NORM_EPS):
    x32 = x.astype(jnp.float32)
    y = x32 * lax.rsqrt(jnp.mean(x32 * x32, axis=-1, keepdims=True) + eps)
    return y.astype(x.dtype)


def _modulate(x, shift, scale):
    return _rmsnorm(x) * (1.0 + scale[:, None, :]) + shift[:, None, :]


def _token_shift(u):
    return jnp.pad(u[:, :-1], ((0, 0), (1, 0), (0, 0)))


def _rope_tables(positions):
    inv_freq = ROPE_THETA ** (-jnp.arange(0, MLA_ROPE_DIM, 2, dtype=jnp.float32) / MLA_ROPE_DIM)
    ang = positions.astype(jnp.float32)[..., None] * inv_freq
    return jnp.cos(ang), jnp.sin(ang)


def _apply_rope(x, cos, sin):
    x1, x2 = jnp.split(x, 2, axis=-1)
    return jnp.concatenate([x1 * cos - x2 * sin, x2 * cos + x1 * sin], axis=-1).astype(x.dtype)


def _rwkv7_scan(r, w, k, v, a, b):
    B, S, H, N = r.shape
    xs = tuple(jnp.moveaxis(t.astype(jnp.float32), 1, 0) for t in (r, w, k, v, a, b))

    def step(St, inp):
        r_t, w_t, k_t, v_t, a_t, b_t = inp
        sa = jnp.einsum('bhij,bhj->bhi', St, a_t)
        St = St * w_t[:, :, None, :] + sa[..., None] * b_t[:, :, None, :] + v_t[..., None] * k_t[:, :, None, :]
        return St, jnp.einsum('bhij,bhj->bhi', St, r_t)

    _, y = lax.scan(step, jnp.zeros((B, H, N, N), jnp.float32), xs)
    return jnp.moveaxis(y, 0, 1).astype(r.dtype)


def _rwkv7_branch(u, mu, w0, w2, a0, a2, g2, k_k, k_a, r_k, ln_w, ln_b):
    B, S, _ = u.shape
    H, N = RWKV_HEADS, RWKV_HEAD_DIM
    u = u + (_token_shift(u) - u) * mu
    r, k, v, wl, al, gl = _split(u, RW_SIZES)
    w = -jax.nn.softplus(-(w0 + jnp.tanh(wl) @ w2)) - 0.5
    decay = jnp.exp(-jnp.exp(w.astype(jnp.float32)))
    a = jax.nn.sigmoid(a0 + al @ a2)
    g = jax.nn.sigmoid(gl) @ g2
    heads = lambda t: t.reshape(B, S, H, N)
    r, k, v, a, decay = heads(r), heads(k), heads(v), heads(a), heads(decay)
    kk = (k * k_k.reshape(H, N)).astype(jnp.float32)
    kk = kk / jnp.maximum(jnp.sqrt(jnp.sum(kk * kk, axis=-1, keepdims=True)), 1e-12)
    kk = kk.astype(k.dtype)
    k = k * (1.0 + (a - 1.0) * k_a.reshape(H, N))
    y = _rwkv7_scan(r, decay, k, v, -kk, kk * a)
    y32 = y.astype(jnp.float32)
    mean = jnp.mean(y32, axis=-1, keepdims=True)
    var = jnp.mean((y32 - mean) ** 2, axis=-1, keepdims=True)
    y = ((y32 - mean) * lax.rsqrt(var + RWKV_LN_EPS)).astype(y.dtype)
    y = y * ln_w.reshape(H, N) + ln_b.reshape(H, N)
    y = y + jnp.sum(r * k * r_k, axis=-1, keepdims=True) * v
    return y.reshape(B, S, H * N) * g


def _gla_chunked(q, k, v, log_f):
    B, S, H, DK = q.shape
    NC = S // HGRN_CHUNK
    to_chunks = lambda t: t.astype(jnp.float32).reshape(B, NC, HGRN_CHUNK, H, -1).transpose(1, 0, 3, 2, 4)
    xs = (to_chunks(q), to_chunks(k), to_chunks(v), to_chunks(log_f))
    causal = jnp.tril(jnp.ones((HGRN_CHUNK, HGRN_CHUNK), bool))

    def step(St, inp):
        q_c, k_c, v_c, g_c = inp
        bcum = jnp.cumsum(g_c, axis=2)
        o_inter = jnp.einsum('bhtd,bhde->bhte', q_c * jnp.exp(bcum), St)
        diff = bcum[:, :, :, None, :] - bcum[:, :, None, :, :]
        dec = jnp.exp(jnp.where(causal[:, :, None], diff, NEG_BIG))
        A = jnp.einsum('bhtsd,bhsd->bhts', q_c[:, :, :, None, :] * dec, k_c)
        o_intra = jnp.einsum('bhts,bhse->bhte', A, v_c)
        b_last = bcum[:, :, -1:, :]
        St = jnp.exp(b_last[:, :, 0, :])[..., None] * St + jnp.einsum('bhsd,bhse->bhde', k_c * jnp.exp(b_last - bcum), v_c)
        return St, o_inter + o_intra

    _, o = lax.scan(step, jnp.zeros((B, H, DK, v.shape[-1]), jnp.float32), xs)
    return o.transpose(1, 0, 3, 2, 4).reshape(B, S, H, v.shape[-1])


def _hgrn2_branch(u, lb, norm_w):
    B, S, _ = u.shape
    H = HGRN_HEADS
    q, fz, i, og = _split(u, HG_SIZES)
    fz32 = fz.astype(jnp.float32).reshape(B, S, H, HGRN_HEAD_DIM)
    lb = lb.reshape(H, HGRN_HEAD_DIM)
    f = lb + (1.0 - lb) * jax.nn.sigmoid(fz32)
    log_f = jnp.log(jnp.maximum(f, F_FLOOR))
    key = (1.0 - lb) * jax.nn.sigmoid(-fz32)
    o = _gla_chunked(q.reshape(B, S, H, HGRN_HEAD_DIM), key, i.reshape(B, S, H, HGRN_VALUE_DIM), log_f)
    o = _rmsnorm(o).astype(u.dtype) * norm_w * jax.nn.silu(og.reshape(B, S, H, HGRN_VALUE_DIM))
    return o.reshape(B, S, H * HGRN_VALUE_DIM)


def _mla_branch(u, cos, sin, q_norm_w, w_uq, kv_norm_w, w_ukv):
    B, S, _ = u.shape
    H = MLA_HEADS
    cq, ckv, k_rope = _split(u, MLA_SIZES)
    q = ((_rmsnorm(cq) * q_norm_w) @ w_uq).reshape(B, S, H, MLA_NOPE_DIM + MLA_ROPE_DIM)
    q_nope, q_rope = jnp.split(q, [MLA_NOPE_DIM], axis=-1)
    kv = ((_rmsnorm(ckv) * kv_norm_w) @ w_ukv).reshape(B, S, H, MLA_NOPE_DIM + MLA_V_DIM)
    k_nope, v = jnp.split(kv, [MLA_NOPE_DIM], axis=-1)
    q_rope = _apply_rope(q_rope, cos[:, :, None, :], sin[:, :, None, :])
    k_rope = _apply_rope(k_rope, cos, sin)
    NB = S // Q_BLOCK
    to_blocks = lambda t: t.reshape(B, NB, Q_BLOCK, H, -1).swapaxes(0, 1)
    k_pos = jnp.arange(S, dtype=jnp.int32)

    def attend(args):
        blk, qn, qr = args
        s = jnp.einsum('bqhd,bkhd->bhqk', qn, k_nope) + jnp.einsum('bqhd,bkd->bhqk', qr, k_rope)
        s = s.astype(jnp.float32) * MLA_SCALE
        q_pos = blk * Q_BLOCK + jnp.arange(Q_BLOCK, dtype=jnp.int32)
        s = jnp.where(k_pos[None, :] <= q_pos[:, None], s, NEG_BIG)
        p = jax.nn.softmax(s, axis=-1).astype(v.dtype)
        return jnp.einsum('bhqk,bkhd->bqhd', p, v)

    o = lax.map(attend, (jnp.arange(NB, dtype=jnp.int32), to_blocks(q_nope), to_blocks(q_rope)))
    return o.swapaxes(0, 1).reshape(B, S, H * MLA_V_DIM)


def _hybrid_mixer(h, cos, sin, lb, w_in, rw_mu, rw_w0, rw_w2, rw_a0, rw_a2, rw_g2, rw_k_k, rw_k_a, rw_r_k,
                  rw_ln_w, rw_ln_b, hg_norm_w, q_norm_w, w_uq, kv_norm_w, w_ukv, proj_a, proj_b, proj_c, w_out):
    B, S, _ = h.shape
    u = h @ w_in
    u_rw, u_hg, u_mla, u_gate = _split(u, IN_SIZES)
    o_a = _rwkv7_branch(u_rw, rw_mu, rw_w0, rw_w2, rw_a0, rw_a2, rw_g2, rw_k_k, rw_k_a, rw_r_k, rw_ln_w, rw_ln_b)
    o_b = _hgrn2_branch(u_hg, lb, hg_norm_w)
    o_c = _mla_branch(u_mla, cos, sin, q_norm_w, w_uq, kv_norm_w, w_ukv)
    gates = jax.nn.sigmoid(u_gate).reshape(B, S, N_BRANCHES, D_MODEL)
    y = gates[:, :, 0] * (o_a @ proj_a) + gates[:, :, 1] * (o_b @ proj_b) + gates[:, :, 2] * (o_c @ proj_c)
    return y @ w_out


def _conv_glu_ffn(h, w_up, conv_w, conv_b, w_down):
    u = h @ w_up
    u = lax.conv_general_dilated(u, conv_w[:, None, :].astype(u.dtype), window_strides=(1,),
                                 padding=[(CONV_WIDTH - 1, 0)], dimension_numbers=('NWC', 'WIO', 'NWC'),
                                 feature_group_count=2 * D_FF) + conv_b
    val, gate = jnp.split(u, 2, axis=-1)
    return (jax.nn.silu(gate) * val) @ w_down


def setup_inputs(seed: int = 0) -> dict:
    key = jax.random.key(seed)
    ks = iter(jax.random.split(key, 40))

    def nrm(shape, scale):
        return jax.random.normal(next(ks), shape, jnp.float32) * scale

    L, D = DEPTH, D_MODEL
    x = nrm((BATCH, SEQ, D), 1.0)
    c = nrm((BATCH, D), 1.0)
    offsets = jax.random.randint(next(ks), (BATCH, 1), 0, 4096, dtype=jnp.int32)
    positions = offsets + jnp.arange(SEQ, dtype=jnp.int32)[None, :]
    return {
        'x': x, 'c': c, 'positions': positions,
        'ada_w': nrm((L, D, 6 * D), 0.5 * D ** -0.5),
        'ada_b': nrm((L, 6 * D), 0.05),
        'w_in': nrm((L, D, IN_WIDTH), D ** -0.5),
        'rwkv_mu': jax.random.uniform(next(ks), (L, RW_COLS), jnp.float32, 0.0, 1.0),
        'rwkv_w0': nrm((L, RWKV_WIDTH), 1.0),
        'rwkv_w2': nrm((L, RWKV_DECAY_LORA, RWKV_WIDTH), 0.5 * RWKV_DECAY_LORA ** -0.5),
        'rwkv_a0': nrm((L, RWKV_WIDTH), 0.5),
        'rwkv_a2': nrm((L, RWKV_AAA_LORA, RWKV_WIDTH), 0.5 * RWKV_AAA_LORA ** -0.5),
        'rwkv_g2': nrm((L, RWKV_GATE_LORA, RWKV_WIDTH), RWKV_GATE_LORA ** -0.5),
        'rwkv_k_k': 0.85 + nrm((L, RWKV_WIDTH), 0.05),
        'rwkv_k_a': 1.0 + nrm((L, RWKV_WIDTH), 0.05),
        'rwkv_r_k': nrm((L, RWKV_HEADS, RWKV_HEAD_DIM), 0.1),
        'rwkv_ln_w': 1.0 + nrm((L, RWKV_WIDTH), 0.05),
        'rwkv_ln_b': nrm((L, RWKV_WIDTH), 0.02),
        'hgrn_lb': 1.0 + nrm((L, HGRN_WIDTH), 0.1),
        'hgrn_norm_w': 1.0 + nrm((L, HGRN_VALUE_DIM), 0.05),
        'mla_q_norm_w': 1.0 + nrm((L, MLA_Q_RANK), 0.05),
        'mla_w_uq': nrm((L, MLA_Q_RANK, MLA_HEADS * (MLA_NOPE_DIM + MLA_ROPE_DIM)), MLA_Q_RANK ** -0.5),
        'mla_kv_norm_w': 1.0 + nrm((L, MLA_KV_RANK), 0.05),
        'mla_w_ukv': nrm((L, MLA_KV_RANK, MLA_HEADS * (MLA_NOPE_DIM + MLA_V_DIM)), MLA_KV_RANK ** -0.5),
        'branch_proj_a': nrm((L, RWKV_WIDTH, D), RWKV_WIDTH ** -0.5),
        'branch_proj_b': nrm((L, HGRN_WIDTH, D), HGRN_WIDTH ** -0.5),
        'branch_proj_c': nrm((L, MLA_WIDTH, D), MLA_WIDTH ** -0.5),
        'w_out': nrm((L, D, D), D ** -0.5),
        'ffn_w_up': nrm((L, D, 2 * D_FF), D ** -0.5),
        'ffn_conv_w': nrm((L, CONV_WIDTH, 2 * D_FF), 0.5),
        'ffn_conv_b': nrm((L, 2 * D_FF), 0.02),
        'ffn_w_down': nrm((L, D_FF, D), D_FF ** -0.5),
        'final_norm_w': 1.0 + nrm((D,), 0.05),
    }


def reference(x, c, positions, ada_w, ada_b, w_in, rwkv_mu, rwkv_w0, rwkv_w2, rwkv_a0, rwkv_a2, rwkv_g2,
              rwkv_k_k, rwkv_k_a, rwkv_r_k, rwkv_ln_w, rwkv_ln_b, hgrn_lb, hgrn_norm_w, mla_q_norm_w, mla_w_uq,
              mla_kv_norm_w, mla_w_ukv, branch_proj_a, branch_proj_b, branch_proj_c, w_out, ffn_w_up,
              ffn_conv_w, ffn_conv_b, ffn_w_down, final_norm_w):
    p = jax.nn.softmax(hgrn_lb.astype(jnp.float32), axis=0)
    lower_bounds = jnp.cumsum(p, axis=0) - p[0:1]
    cos, sin = _rope_tables(positions)
    cond = jax.nn.silu(c)
    B = x.shape[0]
    for l in range(DEPTH):
        mod = (cond @ ada_w[l] + ada_b[l]).reshape(B, 6, D_MODEL)
        sh1, sc1, g1, sh2, sc2, g2 = [mod[:, j] for j in range(6)]
        h = _modulate(x, sh1, sc1)
        x = x + g1[:, None, :] * _hybrid_mixer(
            h, cos, sin, lower_bounds[l], w_in[l], rwkv_mu[l], rwkv_w0[l], rwkv_w2[l], rwkv_a0[l], rwkv_a2[l],
            rwkv_g2[l], rwkv_k_k[l], rwkv_k_a[l], rwkv_r_k[l], rwkv_ln_w[l], rwkv_ln_b[l], hgrn_norm_w[l],
            mla_q_norm_w[l], mla_w_uq[l], mla_kv_norm_w[l], mla_w_ukv[l], branch_proj_a[l], branch_proj_b[l],
            branch_proj_c[l], w_out[l])
        h = _modulate(x, sh2, sc2)
        x = x + g2[:, None, :] * _conv_glu_ffn(h, ffn_w_up[l], ffn_conv_w[l], ffn_conv_b[l], ffn_w_down[l])
    return _rmsnorm(x) * final_norm_w
```

```python
import functools
import math

import numpy as np
import jax
import jax.numpy as jnp
from jax import lax
from jax.experimental import pallas as pl
from jax.experimental.pallas import tpu as pltpu

F32 = jnp.float32
BF16 = jnp.bfloat16
HI = lax.Precision.HIGHEST

D_MODEL = 1024
NORM_EPS = 1e-6

RW_HEADS = 8
RW_N = 64
RW_WIDTH = RW_HEADS * RW_N
RW_COLS = 1792
RW_LN_EPS = 64e-5
RW_CHUNK = 64

HG_HEADS = 4
HG_DK = 128
HG_WIDTH = HG_HEADS * HG_DK
HG_COLS = 2048
HG_SUB = 16
F_FLOOR = 1e-30

MLA_HEADS = 8
MLA_Q_RANK = 256
MLA_KV_RANK = 128
MLA_NOPE = 64
MLA_ROPE = 32
MLA_V = 64
MLA_PAD = 128
MLA_SEG = 640
MLA_SCALE = (MLA_NOPE + MLA_ROPE) ** -0.5
ROPE_THETA = 10000.0
NEG_BIG = -1e30

D_FF = 2816
GATE_COLS = 3 * D_MODEL

VMEM_LIMIT = 56 * 1024 * 1024


def _cparams(sem):
    return pltpu.CompilerParams(dimension_semantics=sem, vmem_limit_bytes=VMEM_LIMIT)


def _dot(a, b, precision=None):
    return jnp.dot(a, b, precision=precision, preferred_element_type=F32)


def _dot_nt(a, b, precision=None):
    return lax.dot_general(a, b, (((1,), (1,)), ((), ())), precision=precision, preferred_element_type=F32)


def _dot_tn(a, b, precision=None):
    return lax.dot_general(a, b, (((0,), (0,)), ((), ())), precision=precision, preferred_element_type=F32)


def _rms(x):
    return x * lax.rsqrt(jnp.mean(x * x, axis=-1, keepdims=True) + NORM_EPS)


def _adaln_kernel(c_ref, w_ref, b_ref, o_ref):
    c = c_ref[...]
    cond = c * jax.nn.sigmoid(c)
    o_ref[0] = _dot(cond, w_ref[0], HI) + b_ref[0]


def _adaln(c, ada_w, ada_b):
    L, D, N = ada_w.shape
    B = c.shape[0]
    tn = 1536
    return pl.pallas_call(
        _adaln_kernel,
        out_shape=jax.ShapeDtypeStruct((L, B, N), F32),
        grid=(L, N // tn),
        in_specs=[
            pl.BlockSpec((B, D), lambda l, j: (0, 0)),
            pl.BlockSpec((1, D, tn), lambda l, j: (l, 0, j)),
            pl.BlockSpec((1, 1, tn), lambda l, j: (l, 0, j)),
        ],
        out_specs=pl.BlockSpec((1, B, tn), lambda l, j: (l, 0, j)),
        compiler_params=_cparams(("arbitrary", "arbitrary")),
        name="adaln",
    )(c, ada_w, ada_b.reshape(L, 1, N))


def _modmm_kernel(x_ref, sh_ref, sc_ref, w_ref, o_ref, h_ref):
    @pl.when(pl.program_id(1) == 0)
    def _():
        h = _rms(x_ref[...]) * (1.0 + sc_ref[0]) + sh_ref[0]
        h_ref[...] = h.astype(BF16)

    o_ref[...] = _dot(h_ref[...], w_ref[...]).astype(o_ref.dtype)


def _modmm(x, shift, scale, w, S, tm, tn, name):
    T, D = x.shape
    N = w.shape[1]
    tm = min(tm, S)
    return pl.pallas_call(
        _modmm_kernel,
        out_shape=jax.ShapeDtypeStruct((T, N), F32),
        grid=(T // tm, N // tn),
        in_specs=[
            pl.BlockSpec((tm, D), lambda i, j: (i, 0)),
            pl.BlockSpec((1, 1, D), lambda i, j: ((i * tm) // S, 0, 0)),
            pl.BlockSpec((1, 1, D), lambda i, j: ((i * tm) // S, 0, 0)),
            pl.BlockSpec((D, tn), lambda i, j: (0, j)),
        ],
        out_specs=pl.BlockSpec((tm, tn), lambda i, j: (i, j)),
        scratch_shapes=[pltpu.VMEM((tm, D), BF16)],
        compiler_params=_cparams(("arbitrary", "arbitrary")),
        name=name,
    )(x, shift, scale, w)


def _rwkv_prep_kernel(u_ref, mu_ref, w0_ref, w2_ref, a0_ref, a2_ref, g2_ref, kk_ref, ka_ref, bd_ref,
                      r_out, k_out, v_out, lw_out, kn_out, bn_out, g_out, carry_ref, *, blocks_per_seq):
    @pl.when(pl.program_id(0) % blocks_per_seq == 0)
    def _():
        carry_ref[...] = jnp.zeros_like(carry_ref)

    u = u_ref[...]
    tm = u.shape[0]
    row = lax.broadcasted_iota(jnp.int32, u.shape, 0)
    prev = jnp.where(row == 0, carry_ref[...], pltpu.roll(u, 1, 0))
    carry_ref[...] = u[tm - 1:tm, :]
    xm = u + (prev - u) * mu_ref[...]
    W = RW_WIDTH
    r = xm[:, 0:W]
    k = xm[:, W:2 * W]
    v = xm[:, 2 * W:3 * W]
    la = xm[:, 3 * W:3 * W + 128]
    gl = xm[:, 3 * W + 128:3 * W + 256]
    w = -jax.nn.softplus(-(w0_ref[...] + _dot(jnp.tanh(la), w2_ref[...], HI))) - 0.5
    lw = -jnp.exp(w)
    a = jax.nn.sigmoid(a0_ref[...] + _dot(la, a2_ref[...], HI))
    g = _dot(jax.nn.sigmoid(gl), g2_ref[...], HI)
    kk = k * kk_ref[...]
    ss = _dot(kk * kk, bd_ref[...], HI)
    kn = kk / jnp.maximum(jnp.sqrt(ss), 1e-12)
    k2 = k * (1.0 + (a - 1.0) * ka_ref[...])
    r_out[...] = r
    k_out[...] = k2
    v_out[...] = v
    lw_out[...] = lw
    kn_out[...] = kn
    bn_out[...] = kn * a
    g_out[...] = g


def _rwkv_prep(u_rw, mu, w0, w2p, a0, a2p, g2, k_k, k_a, bd, S, tm):
    T = u_rw.shape[0]
    tm = min(tm, S)
    W = RW_WIDTH
    full = lambda shape: pl.BlockSpec(shape, lambda i: (0,) * len(shape))
    row_spec = pl.BlockSpec((tm, W), lambda i: (i, 0))
    return pl.pallas_call(
        functools.partial(_rwkv_prep_kernel, blocks_per_seq=S // tm),
        out_shape=[jax.ShapeDtypeStruct((T, W), F32)] * 7,
        grid=(T // tm,),
        in_specs=[
            pl.BlockSpec((tm, RW_COLS), lambda i: (i, 0)),
            full((1, RW_COLS)), full((1, W)), full((128, W)), full((1, W)), full((128, W)), full((128, W)),
            full((1, W)), full((1, W)), full((W, W)),
        ],
        out_specs=[row_spec] * 7,
        scratch_shapes=[pltpu.VMEM((1, RW_COLS), F32)],
        compiler_params=_cparams(("arbitrary",)),
        name="rwkv_prep",
    )(u_rw, mu, w0, w2p, a0, a2p, g2, k_k, k_a, bd)


def _rwkv_scan_kernel(r_ref, k_ref, v_ref, lw_ref, kn_ref, bn_ref, g_ref, lnw_ref, lnb_ref, rk_ref, tri_ref, bd_ref,
                      o_ref, s_ref, y_ref):
    @pl.when(pl.program_id(1) == 0)
    def _():
        s_ref[...] = jnp.zeros_like(s_ref)

    C = RW_CHUNK
    N = RW_N
    r = r_ref[...]
    k = k_ref[...]
    v = v_ref[...]
    lw = lw_ref[...]
    kn = kn_ref[...]
    bn = bn_ref[...]
    c = _dot(tri_ref[...], lw, HI)
    c_end = c[C - 1:C, :]
    e_neg = jnp.exp(-c)
    e_end = jnp.exp(c_end - c)
    a_t = -kn * jnp.exp(c - lw)
    b_t = bn * e_neg
    k_t = k * e_neg
    r_t = r * jnp.exp(c)
    b_h = bn * e_end
    k_h = k * e_end
    g_end = jnp.exp(c_end)

    row = lax.broadcasted_iota(jnp.int32, (C, C), 0)
    col = lax.broadcasted_iota(jnp.int32, (C, C), 1)
    strict = row > col
    incl = row >= col
    eye = (row == col).astype(F32)

    for h in range(RW_HEADS):
        sl = slice(h * N, (h + 1) * N)
        a_s, r_s, v_s = a_t[:, sl], r_t[:, sl], v[:, sl]
        m4 = _dot_nt(jnp.concatenate([a_s, r_s], axis=0), jnp.concatenate([b_t[:, sl], k_t[:, sl]], axis=0), HI)
        ab = jnp.where(strict, m4[0:C, 0:C], 0.0)
        ak = jnp.where(strict, m4[0:C, C:2 * C], 0.0)
        rb = jnp.where(incl, m4[C:2 * C, 0:C], 0.0)
        rk = jnp.where(incl, m4[C:2 * C, C:2 * C], 0.0)
        npow = ab
        tinv = eye + ab
        for _ in range(5):
            npow = _dot(npow, npow, HI)
            tinv = tinv + _dot(tinv, npow, HI)
        s0 = s_ref[h]
        x = _dot_nt(a_s, s0, HI) + _dot(ak, v_s, HI)
        u = _dot(tinv, x, HI)
        y = _dot_nt(r_s, s0, HI) + _dot(rb, u, HI) + _dot(rk, v_s, HI)
        s_new = s0 * g_end[:, sl] + _dot_tn(jnp.concatenate([u, v_s], axis=0),
                                             jnp.concatenate([b_h[:, sl], k_h[:, sl]], axis=0), HI)
        s_ref[h] = s_new
        y_ref[:, sl] = y

    y = y_ref[...]
    bd = bd_ref[...]
    mean = _dot(y, bd, HI) * (1.0 / N)
    d = y - mean
    var = _dot(d * d, bd, HI) * (1.0 / N)
    yn = d * lax.rsqrt(var + RW_LN_EPS) * lnw_ref[...] + lnb_ref[...]
    bonus = _dot(r * k * rk_ref[...], bd, HI)
    o_ref[...] = (yn + bonus * v) * g_ref[...]


def _rwkv_scan(r, k, v, lw, kn, bn, g, ln_w, ln_b, r_k, tri, bd, B, S):
    T, W = r.shape
    C = RW_CHUNK
    NC = S // C
    blk = pl.BlockSpec((C, W), lambda b, c: (b * NC + c, 0))
    full = lambda shape: pl.BlockSpec(shape, lambda b, c: (0,) * len(shape))
    return pl.pallas_call(
        _rwkv_scan_kernel,
        out_shape=jax.ShapeDtypeStruct((T, W), F32),
        grid=(B, NC),
        in_specs=[blk] * 7 + [full((1, W))] * 3 + [full((C, C)), full((W, W))],
        out_specs=blk,
        scratch_shapes=[pltpu.VMEM((RW_HEADS, RW_N, RW_N), F32), pltpu.VMEM((C, W), F32)],
        compiler_params=_cparams(("arbitrary", "arbitrary")),
        name="rwkv_scan",
    )(r, k, v, lw, kn, bn, g, ln_w, ln_b, r_k, tri, bd)


def _hgrn_kernel(q_ref, fz_ref, i_ref, og_ref, lb_ref, nw_ref, tri_ref, o_ref, st_ref, *, layer):
    @pl.when(pl.program_id(2) == 0)
    def _():
        st_ref[...] = jnp.zeros_like(st_ref)

    lbraw = lb_ref[...]
    e = jnp.exp(lbraw - jnp.max(lbraw, axis=0, keepdims=True))
    p = e / jnp.sum(e, axis=0, keepdims=True)
    lb = jnp.zeros((1, p.shape[1]), F32)
    for l in range(1, layer + 1):
        lb = lb + p[l:l + 1, :]
    fz = fz_ref[...]
    f = lb + (1.0 - lb) * jax.nn.sigmoid(fz)
    g = jnp.log(jnp.maximum(f, F_FLOOR))
    key = (1.0 - lb) * jax.nn.sigmoid(-fz)
    bcum = _dot(tri_ref[...], g, HI)
    q = q_ref[...]
    v = i_ref[...]
    qe = q * jnp.exp(bcum)
    rt = q.shape[0]
    n = HG_SUB
    rowi = lax.broadcasted_iota(jnp.int32, (n, HG_DK), 0)
    st = st_ref[...]
    outs = []
    for cidx in range(rt // n):
        rs = slice(cidx * n, (cidx + 1) * n)
        bc, qc, kc, vc = bcum[rs], q[rs], key[rs], v[rs]
        b_last = bc[n - 1:n, :]
        o = _dot_nt(qe[rs].astype(BF16), st.astype(BF16))
        for s in range(n):
            dec = jnp.exp(jnp.where(rowi >= s, bc - bc[s:s + 1, :], NEG_BIG))
            o = o + jnp.sum(qc * dec * kc[s:s + 1, :], axis=-1, keepdims=True) * vc[s:s + 1, :]
        kd = kc * jnp.exp(b_last - bc)
        st = st * jnp.exp(b_last) + _dot_tn(vc.astype(BF16), kd.astype(BF16))
        outs.append(o)
    st_ref[...] = st
    o = jnp.concatenate(outs, axis=0)
    og = og_ref[...]
    o_ref[...] = _rms(o) * nw_ref[...] * (og * jax.nn.sigmoid(og))


def _hgrn(u_hg, lb_raw, norm_w, tri, layer, B, S, rt):
    T = u_hg.shape[0]
    rt = min(rt, S)
    nR = S // rt
    L = lb_raw.shape[0]
    H = HG_HEADS

    def col(off):
        return pl.BlockSpec((rt, HG_DK), lambda b, h, i: (b * nR + i, off + h))

    return pl.pallas_call(
        functools.partial(_hgrn_kernel, layer=layer),
        out_shape=jax.ShapeDtypeStruct((T, HG_WIDTH), F32),
        grid=(B, H, nR),
        in_specs=[
            col(0), col(H), col(2 * H), col(3 * H),
            pl.BlockSpec((L, HG_DK), lambda b, h, i: (0, h)),
            pl.BlockSpec((1, HG_DK), lambda b, h, i: (0, 0)),
            pl.BlockSpec((rt, rt), lambda b, h, i: (0, 0)),
        ],
        out_specs=pl.BlockSpec((rt, HG_DK), lambda b, h, i: (b * nR + i, h)),
        scratch_shapes=[pltpu.VMEM((HG_DK, HG_DK), F32)],
        compiler_params=_cparams(("arbitrary", "arbitrary", "arbitrary")),
        name="hgrn",
    )(u_hg, u_hg, u_hg, u_hg, lb_raw, norm_w, tri)


def _mla_prep_kernel(u_ref, pos_ref, invf_ref, qnw_ref, kvnw_ref, wq_ref, wqs_ref, wk_ref, wv_ref,
                     q_out, k_out, v_out):
    u = u_ref[...]
    cq = u[:, 0:MLA_Q_RANK]
    ckv = u[:, MLA_Q_RANK:MLA_Q_RANK + MLA_KV_RANK]
    kr = u[:, 384:512]
    krs = u[:, 512:640]
    ang = pos_ref[...].astype(F32) * invf_ref[...]
    lane = lax.broadcasted_iota(jnp.int32, ang.shape, 1)
    cosv = jnp.cos(ang)
    sinv = jnp.sin(ang)
    cos_t = jnp.where(lane < MLA_NOPE, 1.0, jnp.where(lane < MLA_NOPE + MLA_ROPE, cosv, 0.0))
    sin_t = jnp.where(lane < MLA_NOPE, 0.0,
                      jnp.where(lane < MLA_NOPE + MLA_ROPE // 2, -sinv,
                                jnp.where(lane < MLA_NOPE + MLA_ROPE, sinv, 0.0)))
    cqn = (_rms(cq) * qnw_ref[...]).astype(BF16)
    kvn = (_rms(ckv) * kvnw_ref[...]).astype(BF16)
    qm = _dot(cqn, wq_ref[...])
    qs = _dot(cqn, wqs_ref[...])
    kn = _dot(kvn, wk_ref[...])
    vv = _dot(kvn, wv_ref[...])
    k_rot = kr * cos_t + krs * sin_t
    for h in range(MLA_HEADS):
        sl = slice(h * MLA_PAD, (h + 1) * MLA_PAD)
        q_out[:, sl] = (qm[:, sl] * cos_t + qs[:, sl] * sin_t).astype(BF16)
        k_out[:, sl] = (kn[:, sl] + k_rot).astype(BF16)
    v_out[...] = vv.astype(BF16)


def _mla_prep(u_mla, pos, invf, qnw, kvnw, wq, wqs, wk, wv, tm):
    T = u_mla.shape[0]
    tm = min(tm, T)
    W = MLA_HEADS * MLA_PAD
    full = lambda shape: pl.BlockSpec(shape, lambda i: (0,) * len(shape))
    out_spec = pl.BlockSpec((tm, W), lambda i: (i, 0))
    return pl.pallas_call(
        _mla_prep_kernel,
        out_shape=[jax.ShapeDtypeStruct((T, W), BF16)] * 3,
        grid=(T // tm,),
        in_specs=[
            pl.BlockSpec((tm, MLA_SEG), lambda i: (i, 0)),
            pl.BlockSpec((tm, 1), lambda i: (i, 0)),
            full((1, MLA_PAD)), full((1, MLA_Q_RANK)), full((1, MLA_KV_RANK)),
            full((MLA_Q_RANK, W)), full((MLA_Q_RANK, W)), full((MLA_KV_RANK, W)), full((MLA_KV_RANK, W)),
        ],
        out_specs=[out_spec] * 3,
        compiler_params=_cparams(("arbitrary",)),
        name="mla_prep",
    )(u_mla, pos, invf, qnw, kvnw, wq, wqs, wk, wv)


def _attn_kernel(qi_ref, ki_ref, q_ref, k_ref, v_ref, o_ref, m_ref, l_ref, acc_ref):
    t = pl.program_id(2)
    qi = qi_ref[t]
    ki = ki_ref[t]

    @pl.when(ki == 0)
    def _():
        m_ref[...] = jnp.full_like(m_ref, NEG_BIG)
        l_ref[...] = jnp.zeros_like(l_ref)
        acc_ref[...] = jnp.zeros_like(acc_ref)

    cexp = MLA_SCALE * math.log2(math.e)

    def update(s):
        m_prev = m_ref[...]
        m_new = jnp.maximum(m_prev, jnp.max(s, axis=-1, keepdims=True))
        alpha = jnp.exp2((m_prev - m_new) * cexp)
        p = jnp.exp2((s - m_new) * cexp)
        l_ref[...] = alpha * l_ref[...] + jnp.sum(p, axis=-1, keepdims=True)
        acc_ref[...] = alpha * acc_ref[...] + _dot(p.astype(BF16), v_ref[...])
        m_ref[...] = m_new

    @pl.when(ki < qi)
    def _():
        update(_dot_nt(q_ref[...], k_ref[...]))

    @pl.when(ki == qi)
    def _():
        s = _dot_nt(q_ref[...], k_ref[...])
        row = lax.broadcasted_iota(jnp.int32, s.shape, 0)
        col = lax.broadcasted_iota(jnp.int32, s.shape, 1)
        update(jnp.where(col <= row, s, NEG_BIG))
        o_ref[...] = (acc_ref[...] / l_ref[...]).astype(o_ref.dtype)


def _mla_attn(q, k, v, B, S, blk):
    T, W = q.shape
    blk = min(blk, S)
    nq = S // blk
    qi = np.array([i for i in range(nq) for _ in range(i + 1)], np.int32)
    ki = np.array([j for i in range(nq) for j in range(i + 1)], np.int32)
    grid_spec = pltpu.PrefetchScalarGridSpec(
        num_scalar_prefetch=2,
        grid=(B, MLA_HEADS, len(qi)),
        in_specs=[
            pl.BlockSpec((blk, MLA_PAD), lambda b, h, t, qi, ki: (b * nq + qi[t], h)),
            pl.BlockSpec((blk, MLA_PAD), lambda b, h, t, qi, ki: (b * nq + ki[t], h)),
            pl.BlockSpec((blk, MLA_PAD), lambda b, h, t, qi, ki: (b * nq + ki[t], h)),
        ],
        out_specs=pl.BlockSpec((blk, MLA_PAD), lambda b, h, t, qi, ki: (b * nq + qi[t], h)),
        scratch_shapes=[pltpu.VMEM((blk, 1), F32), pltpu.VMEM((blk, 1), F32), pltpu.VMEM((blk, MLA_PAD), F32)],
    )
    return pl.pallas_call(
        _attn_kernel,
        out_shape=jax.ShapeDtypeStruct((T, W), BF16),
        grid_spec=grid_spec,
        compiler_params=_cparams(("arbitrary", "arbitrary", "arbitrary")),
        name="mla_attn",
    )(jnp.asarray(qi), jnp.asarray(ki), q, k, v)


def _merge_kernel(x_ref, gt_ref, oa_ref, ob_ref, oc_ref, pa_ref, pb_ref, pc_ref, wo_ref, g1_ref, o_ref):
    D = D_MODEL
    gt = gt_ref[...]
    y = jax.nn.sigmoid(gt[:, 0:D]) * _dot(oa_ref[...].astype(BF16), pa_ref[...])
    y = y + jax.nn.sigmoid(gt[:, D:2 * D]) * _dot(ob_ref[...].astype(BF16), pb_ref[...])
    y = y + jax.nn.sigmoid(gt[:, 2 * D:3 * D]) * _dot(oc_ref[...], pc_ref[...])
    o_ref[...] = x_ref[...] + g1_ref[0] * _dot(y.astype(BF16), wo_ref[...])


def _merge(x, u_gate, o_a, o_b, o_c, pa, pb, pc, wo, g1, S, tm):
    T, D = x.shape
    tm = min(tm, S)
    rows = lambda w: pl.BlockSpec((tm, w), lambda i: (i, 0))
    full = lambda shape: pl.BlockSpec(shape, lambda i: (0,) * len(shape))
    return pl.pallas_call(
        _merge_kernel,
        out_shape=jax.ShapeDtypeStruct((T, D), F32),
        grid=(T // tm,),
        in_specs=[
            rows(D), rows(GATE_COLS), rows(RW_WIDTH), rows(HG_WIDTH), rows(MLA_HEADS * MLA_PAD),
            full(pa.shape), full(pb.shape), full(pc.shape), full(wo.shape),
            pl.BlockSpec((1, 1, D), lambda i: ((i * tm) // S, 0, 0)),
        ],
        out_specs=rows(D),
        compiler_params=_cparams(("arbitrary",)),
        name="merge",
    )(x, u_gate, o_a, o_b, o_c, pa, pb, pc, wo, g1)


def _ffn_up_kernel(x_ref, sh_ref, sc_ref, wv_ref, wg_ref, cwv_ref, cwg_ref, cbv_ref, cbg_ref, o_ref,
                   h_ref, carry_ref, *, blocks_per_seq):
    j = pl.program_id(1)

    @pl.when(j == 0)
    def _():
        h = _rms(x_ref[...]) * (1.0 + sc_ref[0]) + sh_ref[0]
        h_ref[...] = h.astype(BF16)

    @pl.when(pl.program_id(0) % blocks_per_seq == 0)
    def _():
        carry_ref[j] = jnp.zeros(carry_ref.shape[1:], F32)

    h = h_ref[...]
    tm = h.shape[0]
    carry = carry_ref[j]

    def conv(u, cw_ref, cb_ref, c2, c1):
        row = lax.broadcasted_iota(jnp.int32, u.shape, 0)
        u1 = jnp.where(row == 0, c1, pltpu.roll(u, 1, 0))
        u2 = jnp.where(row == 0, c2, jnp.where(row == 1, c1, pltpu.roll(u, 2, 0)))
        cw = cw_ref[...]
        return cw[0:1, :] * u2 + cw[1:2, :] * u1 + cw[2:3, :] * u + cb_ref[...]

    uv = _dot(h, wv_ref[...])
    ug = _dot(h, wg_ref[...])
    val = conv(uv, cwv_ref, cbv_ref, carry[0:1, :], carry[1:2, :])
    gate = conv(ug, cwg_ref, cbg_ref, carry[2:3, :], carry[3:4, :])
    carry_ref[j] = jnp.concatenate([uv[tm - 2:tm, :], ug[tm - 2:tm, :]], axis=0)
    o_ref[...] = (gate * jax.nn.sigmoid(gate) * val).astype(o_ref.dtype)


def _ffn_up(x, shift, scale, wv, wg, cwv, cwg, cbv, cbg, S, tm, tn):
    T, D = x.shape
    tm = min(tm, S)
    ncol = D_FF // tn
    return pl.pallas_call(
        functools.partial(_ffn_up_kernel, blocks_per_seq=S // tm),
        out_shape=jax.ShapeDtypeStruct((T, D_FF), BF16),
        grid=(T // tm, ncol),
        in_specs=[
            pl.BlockSpec((tm, D), lambda i, j: (i, 0)),
            pl.BlockSpec((1, 1, D), lambda i, j: ((i * tm) // S, 0, 0)),
            pl.BlockSpec((1, 1, D), lambda i, j: ((i * tm) // S, 0, 0)),
            pl.BlockSpec((D, tn), lambda i, j: (0, j)),
            pl.BlockSpec((D, tn), lambda i, j: (0, j)),
            pl.BlockSpec((3, tn), lambda i, j: (0, j)),
            pl.BlockSpec((3, tn), lambda i, j: (0, j)),
            pl.BlockSpec((1, tn), lambda i, j: (0, j)),
            pl.BlockSpec((1, tn), lambda i, j: (0, j)),
        ],
        out_specs=pl.BlockSpec((tm, tn), lambda i, j: (i, j)),
        scratch_shapes=[pltpu.VMEM((tm, D), BF16), pltpu.VMEM((ncol, 4, tn), F32)],
        compiler_params=_cparams(("arbitrary", "arbitrary")),
        name="ffn_up",
    )(x, shift, scale, wv, wg, cwv, cwg, cbv, cbg)


def _ffn_down_kernel(x_ref, a_ref, w_ref, g2_ref, fw_ref, o_ref, *, final):
    y = x_ref[...] + g2_ref[0] * _dot(a_ref[...], w_ref[...])
    if final:
        y = _rms(y) * fw_ref[...]
    o_ref[...] = y


def _ffn_down(x, act, w_down, g2, final_w, S, tm, final):
    T, D = x.shape
    tm = min(tm, S)
    return pl.pallas_call(
        functools.partial(_ffn_down_kernel, final=final),
        out_shape=jax.ShapeDtypeStruct((T, D), F32),
        grid=(T // tm,),
        in_specs=[
            pl.BlockSpec((tm, D), lambda i: (i, 0)),
            pl.BlockSpec((tm, D_FF), lambda i: (i, 0)),
            pl.BlockSpec((D_FF, D), lambda i: (0, 0)),
            pl.BlockSpec((1, 1, D), lambda i: ((i * tm) // S, 0, 0)),
            pl.BlockSpec((1, D), lambda i: (0, 0)),
        ],
        out_specs=pl.BlockSpec((tm, D), lambda i: (i, 0)),
        compiler_params=_cparams(("arbitrary",)),
        name="ffn_down",
    )(x, act, w_down, g2, final_w)


def _tiles(S):
    return dict(inproj=min(1024, S), rwkv_prep=min(512, S), hgrn=min(128, S), mla_prep=min(512, S),
                attn=min(1024, S), merge=min(512, S), ffn=min(512, S))


def _head_pad(w, per_head, lo, hi, dst, pad=MLA_PAD):
    K = w.shape[0]
    w = w.reshape(K, MLA_HEADS, per_head)[:, :, lo:hi]
    out = jnp.zeros((K, MLA_HEADS, pad), w.dtype)
    out = out.at[:, :, dst:dst + (hi - lo)].set(w)
    return out.reshape(K, MLA_HEADS * pad)


def _swap_halves(w):
    half = w.shape[-1] // 2
    return jnp.concatenate([w[..., half:], w[..., :half]], axis=-1)


def kernel(x, c, positions, ada_w, ada_b, w_in, rwkv_mu, rwkv_w0, rwkv_w2, rwkv_a0, rwkv_a2, rwkv_g2, rwkv_k_k, rwkv_k_a, rwkv_r_k, rwkv_ln_w, rwkv_ln_b, hgrn_lb, hgrn_norm_w, mla_q_norm_w, mla_w_uq, mla_kv_norm_w, mla_w_ukv, branch_proj_a, branch_proj_b, branch_proj_c, w_out, ffn_w_up, ffn_conv_w, ffn_conv_b, ffn_w_down, final_norm_w):
    B, S, D = x.shape
    L = ada_w.shape[0]
    T = B * S
    xt = x.reshape(T, D)
    pos = positions.reshape(T, 1)

    mod = _adaln(c, ada_w, ada_b).reshape(L, B, 6, 1, D)

    inv_freq = ROPE_THETA ** (-jnp.arange(0, MLA_ROPE, 2, dtype=F32) / MLA_ROPE)
    invf = jnp.zeros((1, MLA_PAD), F32).at[0, MLA_NOPE:MLA_NOPE + MLA_ROPE].set(jnp.concatenate([inv_freq, inv_freq]))

    ii = np.arange(RW_WIDTH)
    bd = jnp.asarray((ii[:, None] // RW_N == ii[None, :] // RW_N).astype(np.float32))
    cc = np.arange(RW_CHUNK)
    tri_rw = jnp.asarray((cc[:, None] >= cc[None, :]).astype(np.float32))
    tl = _tiles(S)
    rt_hg = tl['hgrn']
    rr = np.arange(rt_hg)
    tri_hg = jnp.asarray(((rr[:, None] >= rr[None, :]) & (rr[:, None] // HG_SUB == rr[None, :] // HG_SUB)).astype(np.float32))

    o0 = RW_COLS
    o1 = o0 + HG_COLS
    o2 = o1 + MLA_Q_RANK + MLA_KV_RANK + MLA_ROPE
    for l in range(L):
        sh1, sc1, g1, sh2, sc2, g2 = (mod[l, :, j] for j in range(6))
        wl = w_in[l]
        w_rw = wl[:, 0:o0].astype(BF16)
        w_hg = wl[:, o0:o1].astype(BF16)
        w_gate = wl[:, o2:o2 + GATE_COLS].astype(BF16)
        w_kr = wl[:, o2 - MLA_ROPE:o2]
        lanes = jnp.zeros((D, MLA_PAD), F32)
        w_mla = jnp.concatenate([
            wl[:, o1:o1 + MLA_Q_RANK + MLA_KV_RANK],
            lanes.at[:, MLA_NOPE:MLA_NOPE + MLA_ROPE].set(w_kr),
            lanes.at[:, MLA_NOPE:MLA_NOPE + MLA_ROPE].set(_swap_halves(w_kr)),
        ], axis=1).astype(BF16)

        u_rw = _modmm(xt, sh1, sc1, w_rw, S, tl['inproj'], RW_COLS, "inproj_rw")
        u_hg = _modmm(xt, sh1, sc1, w_hg, S, tl['inproj'], HG_COLS, "inproj_hg")
        u_gate = _modmm(xt, sh1, sc1, w_gate, S, tl['inproj'], GATE_COLS // 2, "inproj_gate")
        u_mla = _modmm(xt, sh1, sc1, w_mla, S, tl['inproj'], MLA_SEG, "inproj_mla")

        zpad = jnp.zeros((64, RW_WIDTH), F32)
        r, k2, v, lw, kn, bn, g = _rwkv_prep(
            u_rw, rwkv_mu[l].reshape(1, -1), rwkv_w0[l].reshape(1, -1),
            jnp.concatenate([rwkv_w2[l], zpad], axis=0), rwkv_a0[l].reshape(1, -1),
            jnp.concatenate([zpad, rwkv_a2[l]], axis=0), rwkv_g2[l],
            rwkv_k_k[l].reshape(1, -1), rwkv_k_a[l].reshape(1, -1), bd, S, tl['rwkv_prep'])
        o_a = _rwkv_scan(r, k2, v, lw, kn, bn, g, rwkv_ln_w[l].reshape(1, -1), rwkv_ln_b[l].reshape(1, -1),
                         rwkv_r_k[l].reshape(1, -1), tri_rw, bd, B, S)

        o_b = _hgrn(u_hg, hgrn_lb, hgrn_norm_w[l].reshape(1, -1), tri_hg, l, B, S, rt_hg)

        per_q = MLA_NOPE + MLA_ROPE
        wq = _head_pad(mla_w_uq[l], per_q, 0, per_q, 0).astype(BF16)
        wq_rope = mla_w_uq[l].reshape(MLA_Q_RANK, MLA_HEADS, per_q)[:, :, MLA_NOPE:]
        wqs = _head_pad(_swap_halves(wq_rope).reshape(MLA_Q_RANK, -1), MLA_ROPE, 0, MLA_ROPE, MLA_NOPE).astype(BF16)
        per_kv = MLA_NOPE + MLA_V
        wk = _head_pad(mla_w_ukv[l], per_kv, 0, MLA_NOPE, 0).astype(BF16)
        wv = _head_pad(mla_w_ukv[l], per_kv, MLA_NOPE, per_kv, 0).astype(BF16)
        q_cat, k_cat, v_pad = _mla_prep(u_mla, pos, invf, mla_q_norm_w[l].reshape(1, -1),
                                        mla_kv_norm_w[l].reshape(1, -1), wq, wqs, wk, wv, tl['mla_prep'])
        o_c = _mla_attn(q_cat, k_cat, v_pad, B, S, tl['attn'])

        pc = jnp.zeros((MLA_HEADS, MLA_PAD, D), F32).at[:, :MLA_V, :].set(
            branch_proj_c[l].reshape(MLA_HEADS, MLA_V, D)).reshape(MLA_HEADS * MLA_PAD, D).astype(BF16)
        xt = _merge(xt, u_gate, o_a, o_b, o_c, branch_proj_a[l].astype(BF16), branch_proj_b[l].astype(BF16), pc,
                    w_out[l].astype(BF16), g1, S, tl['merge'])

        wu = ffn_w_up[l]
        cw = ffn_conv_w[l]
        cb = ffn_conv_b[l].reshape(1, -1)
        act = _ffn_up(xt, sh2, sc2, wu[:, :D_FF].astype(BF16), wu[:, D_FF:].astype(BF16),
                      cw[:, :D_FF], cw[:, D_FF:], cb[:, :D_FF], cb[:, D_FF:], S, tl['ffn'], D_FF // 2)
        xt = _ffn_down(xt, act, ffn_w_down[l].astype(BF16), g2, final_norm_w.reshape(1, -1), S, tl['ffn'], l == L - 1)

    return xt.reshape(B, S, D)
```

```python
import functools
import math

import numpy as np
import jax
import jax.numpy as jnp
from jax import lax
from jax.experimental import pallas as pl
from jax.experimental.pallas import tpu as pltpu

F32 = jnp.float32
BF16 = jnp.bfloat16
HI = lax.Precision.HIGHEST

D_MODEL = 1024
NORM_EPS = 1e-6

RW_HEADS = 8
RW_N = 64
RW_WIDTH = RW_HEADS * RW_N
RW_COLS = 1792
RW_LN_EPS = 64e-5
RW_CHUNK = 64

HG_HEADS = 4
HG_DK = 128
HG_WIDTH = HG_HEADS * HG_DK
HG_COLS = 2048
HG_SUB = 16
F_FLOOR = 1e-30

MLA_HEADS = 8
MLA_Q_RANK = 256
MLA_KV_RANK = 128
MLA_NOPE = 64
MLA_ROPE = 32
MLA_V = 64
MLA_PAD = 128
MLA_SEG = 640
MLA_SCALE = (MLA_NOPE + MLA_ROPE) ** -0.5
ROPE_THETA = 10000.0
NEG_BIG = -1e30

D_FF = 2816
GATE_COLS = 3 * D_MODEL

VMEM_LIMIT = 56 * 1024 * 1024


def _cparams(sem):
    return pltpu.CompilerParams(dimension_semantics=sem, vmem_limit_bytes=VMEM_LIMIT)


def _dot(a, b, precision=None):
    return jnp.dot(a, b, precision=precision, preferred_element_type=F32)


def _dot_nt(a, b, precision=None):
    return lax.dot_general(a, b, (((1,), (1,)), ((), ())), precision=precision, preferred_element_type=F32)


def _dot_tn(a, b, precision=None):
    return lax.dot_general(a, b, (((0,), (0,)), ((), ())), precision=precision, preferred_element_type=F32)


def _rms(x):
    return x * lax.rsqrt(jnp.mean(x * x, axis=-1, keepdims=True) + NORM_EPS)


def _adaln_kernel(c_ref, w_ref, b_ref, o_ref):
    c = c_ref[...]
    cond = c * jax.nn.sigmoid(c)
    o_ref[0] = _dot(cond, w_ref[0], HI) + b_ref[0]


def _adaln(c, ada_w, ada_b):
    L, D, N = ada_w.shape
    B = c.shape[0]
    tn = 1536
    return pl.pallas_call(
        _adaln_kernel,
        out_shape=jax.ShapeDtypeStruct((L, B, N), F32),
        grid=(L, N // tn),
        in_specs=[
            pl.BlockSpec((B, D), lambda l, j: (0, 0)),
            pl.BlockSpec((1, D, tn), lambda l, j: (l, 0, j)),
            pl.BlockSpec((1, 1, tn), lambda l, j: (l, 0, j)),
        ],
        out_specs=pl.BlockSpec((1, B, tn), lambda l, j: (l, 0, j)),
        compiler_params=_cparams(("arbitrary", "arbitrary")),
        name="adaln",
    )(c, ada_w, ada_b.reshape(L, 1, N))


def _modmm_kernel(x_ref, sh_ref, sc_ref, w_ref, o_ref, h_ref):
    @pl.when(pl.program_id(1) == 0)
    def _():
        h = _rms(x_ref[...]) * (1.0 + sc_ref[0]) + sh_ref[0]
        h_ref[...] = h.astype(BF16)

    o_ref[...] = _dot(h_ref[...], w_ref[...]).astype(o_ref.dtype)


def _modmm(x, shift, scale, w, S, tm, tn, name):
    T, D = x.shape
    N = w.shape[1]
    tm = min(tm, S)
    return pl.pallas_call(
        _modmm_kernel,
        out_shape=jax.ShapeDtypeStruct((T, N), F32),
        grid=(T // tm, N // tn),
        in_specs=[
            pl.BlockSpec((tm, D), lambda i, j: (i, 0)),
            pl.BlockSpec((1, 1, D), lambda i, j: ((i * tm) // S, 0, 0)),
            pl.BlockSpec((1, 1, D), lambda i, j: ((i * tm) // S, 0, 0)),
            pl.BlockSpec((D, tn), lambda i, j: (0, j)),
        ],
        out_specs=pl.BlockSpec((tm, tn), lambda i, j: (i, j)),
        scratch_shapes=[pltpu.VMEM((tm, D), BF16)],
        compiler_params=_cparams(("arbitrary", "arbitrary")),
        name=name,
    )(x, shift, scale, w)


def _rwkv_prep_kernel(u_ref, mu_ref, w0_ref, w2_ref, a0_ref, a2_ref, g2_ref, kk_ref, ka_ref, bd_ref,
                      r_out, k_out, v_out, lw_out, kn_out, bn_out, g_out, carry_ref, *, blocks_per_seq):
    @pl.when(pl.program_id(0) % blocks_per_seq == 0)
    def _():
        carry_ref[...] = jnp.zeros_like(carry_ref)

    u = u_ref[...]
    tm = u.shape[0]
    row = lax.broadcasted_iota(jnp.int32, u.shape, 0)
    prev = jnp.where(row == 0, carry_ref[...], pltpu.roll(u, 1, 0))
    carry_ref[...] = u[tm - 1:tm, :]
    xm = u + (prev - u) * mu_ref[...]
    W = RW_WIDTH
    r = xm[:, 0:W]
    k = xm[:, W:2 * W]
    v = xm[:, 2 * W:3 * W]
    la = xm[:, 3 * W:3 * W + 128]
    gl = xm[:, 3 * W + 128:3 * W + 256]
    w = -jax.nn.softplus(-(w0_ref[...] + _dot(jnp.tanh(la), w2_ref[...], HI))) - 0.5
    lw = -jnp.exp(w)
    a = jax.nn.sigmoid(a0_ref[...] + _dot(la, a2_ref[...], HI))
    g = _dot(jax.nn.sigmoid(gl), g2_ref[...], HI)
    kk = k * kk_ref[...]
    ss = _dot(kk * kk, bd_ref[...], HI)
    kn = kk / jnp.maximum(jnp.sqrt(ss), 1e-12)
    k2 = k * (1.0 + (a - 1.0) * ka_ref[...])
    r_out[...] = r
    k_out[...] = k2
    v_out[...] = v
    lw_out[...] = lw
    kn_out[...] = kn
    bn_out[...] = kn * a
    g_out[...] = g


def _rwkv_prep(u_rw, mu, w0, w2p, a0, a2p, g2, k_k, k_a, bd, S, tm):
    T = u_rw.shape[0]
    tm = min(tm, S)
    W = RW_WIDTH
    full = lambda shape: pl.BlockSpec(shape, lambda i: (0,) * len(shape))
    row_spec = pl.BlockSpec((tm, W), lambda i: (i, 0))
    return pl.pallas_call(
        functools.partial(_rwkv_prep_kernel, blocks_per_seq=S // tm),
        out_shape=[jax.ShapeDtypeStruct((T, W), F32)] * 7,
        grid=(T // tm,),
        in_specs=[
            pl.BlockSpec((tm, RW_COLS), lambda i: (i, 0)),
            full((1, RW_COLS)), full((1, W)), full((128, W)), full((1, W)), full((128, W)), full((128, W)),
            full((1, W)), full((1, W)), full((W, W)),
        ],
        out_specs=[row_spec] * 7,
        scratch_shapes=[pltpu.VMEM((1, RW_COLS), F32)],
        compiler_params=_cparams(("arbitrary",)),
        name="rwkv_prep",
    )(u_rw, mu, w0, w2p, a0, a2p, g2, k_k, k_a, bd)


def _rwkv_scan_kernel(r_ref, k_ref, v_ref, lw_ref, kn_ref, bn_ref, g_ref, lnw_ref, lnb_ref, rk_ref, tri_ref, bd_ref,
                      o_ref, s_ref, y_ref):
    @pl.when(pl.program_id(0) == 0)
    def _():
        s_ref[...] = jnp.zeros_like(s_ref)

    C = RW_CHUNK
    N = RW_N
    B = r_ref.shape[0]
    tri = tri_ref[...]
    bd = bd_ref[...]
    row = lax.broadcasted_iota(jnp.int32, (C, 2 * C), 0)
    col = lax.broadcasted_iota(jnp.int32, (C, 2 * C), 1) % C
    strict = row > col
    incl = row >= col
    eye = (lax.broadcasted_iota(jnp.int32, (C, C), 0) == lax.broadcasted_iota(jnp.int32, (C, C), 1)).astype(F32)

    wide = []
    for b in range(B):
        r = r_ref[b]
        k = k_ref[b]
        v = v_ref[b]
        lw = lw_ref[b]
        kn = kn_ref[b]
        bn = bn_ref[b]
        lw1 = lw.astype(BF16)
        rem = lw - lw1.astype(F32)
        lw2 = rem.astype(BF16)
        lw3 = (rem - lw2.astype(F32)).astype(BF16)
        c = _dot(tri, lw1) + _dot(tri, lw2) + _dot(tri, lw3)
        c_end = c[C - 1:C, :]
        e_neg = jnp.exp(-c)
        e_end = jnp.exp(c_end - c)
        wide.append(dict(
            a=(-kn * jnp.exp(c - lw)).astype(BF16),
            b=(bn * e_neg).astype(BF16),
            k=(k * e_neg).astype(BF16),
            r=(r * jnp.exp(c)).astype(BF16),
            bh=(bn * e_end).astype(BF16),
            kh=(k * e_end).astype(BF16),
            v=v.astype(BF16),
            g_end=jnp.exp(c_end)))

    chains = [(b, h) for b in range(B) for h in range(RW_HEADS)]
    sl = lambda h: slice(h * N, (h + 1) * N)
    m4 = [_dot_nt(jnp.concatenate([wide[b]['a'][:, sl(h)], wide[b]['r'][:, sl(h)]], axis=0),
                  jnp.concatenate([wide[b]['b'][:, sl(h)], wide[b]['k'][:, sl(h)]], axis=0)) for b, h in chains]
    abk = [jnp.where(strict, m[0:C, :], 0.0) for m in m4]
    rbk = [jnp.where(incl, m[C:2 * C, :], 0.0).astype(BF16) for m in m4]
    npow = [m[:, 0:C] for m in abk]
    tinv = [eye + n for n in npow]
    for _ in range(5):
        nb = [n.astype(BF16) for n in npow]
        npow = [_dot(n, n) for n in nb]
        tinv = [t + _dot(t.astype(BF16), n.astype(BF16)) for t, n in zip(tinv, npow)]
    s0 = [s_ref[b * RW_HEADS + h] for b, h in chains]
    s0b = [s.astype(BF16) for s in s0]
    x = [_dot_nt(wide[b]['a'][:, sl(h)], s) for (b, h), s in zip(chains, s0b)]
    x = [xi + _dot(m[:, C:2 * C].astype(BF16), wide[b]['v'][:, sl(h)]) for (b, h), xi, m in zip(chains, x, abk)]
    u = [_dot(t.astype(BF16), xi.astype(BF16)) for t, xi in zip(tinv, x)]
    uv = [jnp.concatenate([ui.astype(BF16), wide[b]['v'][:, sl(h)]], axis=0) for (b, h), ui in zip(chains, u)]
    y = [_dot_nt(wide[b]['r'][:, sl(h)], s) for (b, h), s in zip(chains, s0b)]
    y = [yi + _dot(m, w) for yi, m, w in zip(y, rbk, uv)]
    snew = [_dot_tn(w, jnp.concatenate([wide[b]['bh'][:, sl(h)], wide[b]['kh'][:, sl(h)]], axis=0))
            for (b, h), w in zip(chains, uv)]
    for (b, h), s, sn, yi in zip(chains, s0, snew, y):
        s_ref[b * RW_HEADS + h] = s * wide[b]['g_end'][:, sl(h)] + sn
        y_ref[b, :, sl(h)] = yi

    for b in range(B):
        r = r_ref[b]
        k = k_ref[b]
        v = v_ref[b]
        y = y_ref[b]
        mean = _dot(y.astype(BF16), bd) * (1.0 / N)
        d = y - mean
        var = _dot((d * d).astype(BF16), bd) * (1.0 / N)
        yn = d * lax.rsqrt(var + RW_LN_EPS) * lnw_ref[...] + lnb_ref[...]
        bonus = _dot((r * k * rk_ref[...]).astype(BF16), bd)
        o_ref[b] = (yn + bonus * v) * g_ref[b]


def _rwkv_scan(r, k, v, lw, kn, bn, g, ln_w, ln_b, r_k, tri, bd, B, S):
    T, W = r.shape
    C = RW_CHUNK
    blk = pl.BlockSpec((B, C, W), lambda c: (0, c, 0))
    full = lambda shape: pl.BlockSpec(shape, lambda c: (0,) * len(shape))
    seq = lambda t: t.reshape(B, S, W)
    out = pl.pallas_call(
        _rwkv_scan_kernel,
        out_shape=jax.ShapeDtypeStruct((B, S, W), F32),
        grid=(S // C,),
        in_specs=[blk] * 7 + [full((1, W))] * 3 + [full((C, C)), full((W, W))],
        out_specs=blk,
        scratch_shapes=[pltpu.VMEM((B * RW_HEADS, RW_N, RW_N), F32), pltpu.VMEM((B, C, W), F32)],
        compiler_params=_cparams(("arbitrary",)),
        name="rwkv_scan",
    )(seq(r), seq(k), seq(v), seq(lw), seq(kn), seq(bn), seq(g), ln_w, ln_b, r_k, tri, bd)
    return out.reshape(T, W)


def _hgrn_kernel(q_ref, fz_ref, i_ref, og_ref, lb_ref, nw_ref, tri_ref, o_ref, st_ref, *, layer):
    @pl.when(pl.program_id(2) == 0)
    def _():
        st_ref[...] = jnp.zeros_like(st_ref)

    lbraw = lb_ref[...]
    e = jnp.exp(lbraw - jnp.max(lbraw, axis=0, keepdims=True))
    p = e / jnp.sum(e, axis=0, keepdims=True)
    lb = jnp.zeros((1, p.shape[1]), F32)
    for l in range(1, layer + 1):
        lb = lb + p[l:l + 1, :]
    fz = fz_ref[...]
    f = lb + (1.0 - lb) * jax.nn.sigmoid(fz)
    g = jnp.log(jnp.maximum(f, F_FLOOR))
    key = (1.0 - lb) * jax.nn.sigmoid(-fz)
    bcum = _dot(tri_ref[...], g, HI)
    q = q_ref[...]
    v = i_ref[...]
    qe = q * jnp.exp(bcum)
    rt = q.shape[0]
    n = HG_SUB
    rowi = lax.broadcasted_iota(jnp.int32, (n, HG_DK), 0)
    st = st_ref[...]
    outs = []
    for cidx in range(rt // n):
        rs = slice(cidx * n, (cidx + 1) * n)
        bc, qc, kc, vc = bcum[rs], q[rs], key[rs], v[rs]
        b_last = bc[n - 1:n, :]
        o = _dot_nt(qe[rs].astype(BF16), st.astype(BF16))
        for s in range(n):
            dec = jnp.exp(jnp.where(rowi >= s, bc - bc[s:s + 1, :], NEG_BIG))
            o = o + jnp.sum(qc * dec * kc[s:s + 1, :], axis=-1, keepdims=True) * vc[s:s + 1, :]
        kd = kc * jnp.exp(b_last - bc)
        st = st * jnp.exp(b_last) + _dot_tn(vc.astype(BF16), kd.astype(BF16))
        outs.append(o)
    st_ref[...] = st
    o = jnp.concatenate(outs, axis=0)
    og = og_ref[...]
    o_ref[...] = _rms(o) * nw_ref[...] * (og * jax.nn.sigmoid(og))


def _hgrn(u_hg, lb_raw, norm_w, tri, layer, B, S, rt):
    T = u_hg.shape[0]
    rt = min(rt, S)
    nR = S // rt
    L = lb_raw.shape[0]
    H = HG_HEADS

    def col(off):
        return pl.BlockSpec((rt, HG_DK), lambda b, h, i: (b * nR + i, off + h))

    return pl.pallas_call(
        functools.partial(_hgrn_kernel, layer=layer),
        out_shape=jax.ShapeDtypeStruct((T, HG_WIDTH), F32),
        grid=(B, H, nR),
        in_specs=[
            col(0), col(H), col(2 * H), col(3 * H),
            pl.BlockSpec((L, HG_DK), lambda b, h, i: (0, h)),
            pl.BlockSpec((1, HG_DK), lambda b, h, i: (0, 0)),
            pl.BlockSpec((rt, rt), lambda b, h, i: (0, 0)),
        ],
        out_specs=pl.BlockSpec((rt, HG_DK), lambda b, h, i: (b * nR + i, h)),
        scratch_shapes=[pltpu.VMEM((HG_DK, HG_DK), F32)],
        compiler_params=_cparams(("arbitrary", "arbitrary", "arbitrary")),
        name="hgrn",
    )(u_hg, u_hg, u_hg, u_hg, lb_raw, norm_w, tri)


def _mla_prep_kernel(u_ref, pos_ref, invf_ref, qnw_ref, kvnw_ref, wq_ref, wqs_ref, wk_ref, wv_ref,
                     q_out, k_out, v_out):
    u = u_ref[...]
    cq = u[:, 0:MLA_Q_RANK]
    ckv = u[:, MLA_Q_RANK:MLA_Q_RANK + MLA_KV_RANK]
    kr = u[:, 384:512]
    krs = u[:, 512:640]
    ang = pos_ref[...].astype(F32) * invf_ref[...]
    lane = lax.broadcasted_iota(jnp.int32, ang.shape, 1)
    cosv = jnp.cos(ang)
    sinv = jnp.sin(ang)
    cos_t = jnp.where(lane < MLA_NOPE, 1.0, jnp.where(lane < MLA_NOPE + MLA_ROPE, cosv, 0.0))
    sin_t = jnp.where(lane < MLA_NOPE, 0.0,
                      jnp.where(lane < MLA_NOPE + MLA_ROPE // 2, -sinv,
                                jnp.where(lane < MLA_NOPE + MLA_ROPE, sinv, 0.0)))
    cqn = (_rms(cq) * qnw_ref[...]).astype(BF16)
    kvn = (_rms(ckv) * kvnw_ref[...]).astype(BF16)
    qm = _dot(cqn, wq_ref[...])
    qs = _dot(cqn, wqs_ref[...])
    kn = _dot(kvn, wk_ref[...])
    vv = _dot(kvn, wv_ref[...])
    k_rot = kr * cos_t + krs * sin_t
    for h in range(MLA_HEADS):
        sl = slice(h * MLA_PAD, (h + 1) * MLA_PAD)
        q_out[:, sl] = (qm[:, sl] * cos_t + qs[:, sl] * sin_t).astype(BF16)
        k_out[:, sl] = (kn[:, sl] + k_rot).astype(BF16)
    v_out[...] = vv.astype(BF16)


def _mla_prep(u_mla, pos, invf, qnw, kvnw, wq, wqs, wk, wv, tm):
    T = u_mla.shape[0]
    tm = min(tm, T)
    W = MLA_HEADS * MLA_PAD
    full = lambda shape: pl.BlockSpec(shape, lambda i: (0,) * len(shape))
    out_spec = pl.BlockSpec((tm, W), lambda i: (i, 0))
    return pl.pallas_call(
        _mla_prep_kernel,
        out_shape=[jax.ShapeDtypeStruct((T, W), BF16)] * 3,
        grid=(T // tm,),
        in_specs=[
            pl.BlockSpec((tm, MLA_SEG), lambda i: (i, 0)),
            pl.BlockSpec((tm, 1), lambda i: (i, 0)),
            full((1, MLA_PAD)), full((1, MLA_Q_RANK)), full((1, MLA_KV_RANK)),
            full((MLA_Q_RANK, W)), full((MLA_Q_RANK, W)), full((MLA_KV_RANK, W)), full((MLA_KV_RANK, W)),
        ],
        out_specs=[out_spec] * 3,
        compiler_params=_cparams(("arbitrary",)),
        name="mla_prep",
    )(u_mla, pos, invf, qnw, kvnw, wq, wqs, wk, wv)


def _attn_kernel(qi_ref, ki_ref, q_ref, k_ref, v_ref, o_ref, m_ref, l_ref, acc_ref):
    t = pl.program_id(2)
    qi = qi_ref[t]
    ki = ki_ref[t]

    @pl.when(ki == 0)
    def _():
        m_ref[...] = jnp.full_like(m_ref, NEG_BIG)
        l_ref[...] = jnp.zeros_like(l_ref)
        acc_ref[...] = jnp.zeros_like(acc_ref)

    cexp = MLA_SCALE * math.log2(math.e)

    def update(s):
        m_prev = m_ref[...]
        m_new = jnp.maximum(m_prev, jnp.max(s, axis=-1, keepdims=True))
        alpha = jnp.exp2((m_prev - m_new) * cexp)
        p = jnp.exp2((s - m_new) * cexp)
        l_ref[...] = alpha * l_ref[...] + jnp.sum(p, axis=-1, keepdims=True)
        acc_ref[...] = alpha * acc_ref[...] + _dot(p.astype(BF16), v_ref[...])
        m_ref[...] = m_new

    @pl.when(ki < qi)
    def _():
        update(_dot_nt(q_ref[...], k_ref[...]))

    @pl.when(ki == qi)
    def _():
        s = _dot_nt(q_ref[...], k_ref[...])
        row = lax.broadcasted_iota(jnp.int32, s.shape, 0)
        col = lax.broadcasted_iota(jnp.int32, s.shape, 1)
        update(jnp.where(col <= row, s, NEG_BIG))
        o_ref[...] = (acc_ref[...] / l_ref[...]).astype(o_ref.dtype)


def _mla_attn(q, k, v, B, S, blk):
    T, W = q.shape
    blk = min(blk, S)
    nq = S // blk
    qi = np.array([i for i in range(nq) for _ in range(i + 1)], np.int32)
    ki = np.array([j for i in range(nq) for j in range(i + 1)], np.int32)
    grid_spec = pltpu.PrefetchScalarGridSpec(
        num_scalar_prefetch=2,
        grid=(B, MLA_HEADS, len(qi)),
        in_specs=[
            pl.BlockSpec((blk, MLA_PAD), lambda b, h, t, qi, ki: (b * nq + qi[t], h)),
            pl.BlockSpec((blk, MLA_PAD), lambda b, h, t, qi, ki: (b * nq + ki[t], h)),
            pl.BlockSpec((blk, MLA_PAD), lambda b, h, t, qi, ki: (b * nq + ki[t], h)),
        ],
        out_specs=pl.BlockSpec((blk, MLA_PAD), lambda b, h, t, qi, ki: (b * nq + qi[t], h)),
        scratch_shapes=[pltpu.VMEM((blk, 1), F32), pltpu.VMEM((blk, 1), F32), pltpu.VMEM((blk, MLA_PAD), F32)],
    )
    return pl.pallas_call(
        _attn_kernel,
        out_shape=jax.ShapeDtypeStruct((T, W), BF16),
        grid_spec=grid_spec,
        compiler_params=_cparams(("arbitrary", "arbitrary", "arbitrary")),
        name="mla_attn",
    )(jnp.asarray(qi), jnp.asarray(ki), q, k, v)


def _merge_kernel(x_ref, gt_ref, oa_ref, ob_ref, oc_ref, pa_ref, pb_ref, pc_ref, wo_ref, g1_ref, o_ref):
    D = D_MODEL
    gt = gt_ref[...]
    y = jax.nn.sigmoid(gt[:, 0:D]) * _dot(oa_ref[...].astype(BF16), pa_ref[...])
    y = y + jax.nn.sigmoid(gt[:, D:2 * D]) * _dot(ob_ref[...].astype(BF16), pb_ref[...])
    y = y + jax.nn.sigmoid(gt[:, 2 * D:3 * D]) * _dot(oc_ref[...], pc_ref[...])
    o_ref[...] = x_ref[...] + g1_ref[0] * _dot(y.astype(BF16), wo_ref[...])


def _merge(x, u_gate, o_a, o_b, o_c, pa, pb, pc, wo, g1, S, tm):
    T, D = x.shape
    tm = min(tm, S)
    rows = lambda w: pl.BlockSpec((tm, w), lambda i: (i, 0))
    full = lambda shape: pl.BlockSpec(shape, lambda i: (0,) * len(shape))
    return pl.pallas_call(
        _merge_kernel,
        out_shape=jax.ShapeDtypeStruct((T, D), F32),
        grid=(T // tm,),
        in_specs=[
            rows(D), rows(GATE_COLS), rows(RW_WIDTH), rows(HG_WIDTH), rows(MLA_HEADS * MLA_PAD),
            full(pa.shape), full(pb.shape), full(pc.shape), full(wo.shape),
            pl.BlockSpec((1, 1, D), lambda i: ((i * tm) // S, 0, 0)),
        ],
        out_specs=rows(D),
        compiler_params=_cparams(("arbitrary",)),
        name="merge",
    )(x, u_gate, o_a, o_b, o_c, pa, pb, pc, wo, g1)


def _ffn_up_kernel(x_ref, sh_ref, sc_ref, wv_ref, wg_ref, cwv_ref, cwg_ref, cbv_ref, cbg_ref, o_ref,
                   h_ref, carry_ref, *, blocks_per_seq):
    j = pl.program_id(1)

    @pl.when(j == 0)
    def _():
        h = _rms(x_ref[...]) * (1.0 + sc_ref[0]) + sh_ref[0]
        h_ref[...] = h.astype(BF16)

    @pl.when(pl.program_id(0) % blocks_per_seq == 0)
    def _():
        carry_ref[j] = jnp.zeros(carry_ref.shape[1:], F32)

    h = h_ref[...]
    tm = h.shape[0]
    carry = carry_ref[j]

    def conv(u, cw_ref, cb_ref, c2, c1):
        row = lax.broadcasted_iota(jnp.int32, u.shape, 0)
        u1 = jnp.where(row == 0, c1, pltpu.roll(u, 1, 0))
        u2 = jnp.where(row == 0, c2, jnp.where(row == 1, c1, pltpu.roll(u, 2, 0)))
        cw = cw_ref[...]
        return cw[0:1, :] * u2 + cw[1:2, :] * u1 + cw[2:3, :] * u + cb_ref[...]

    uv = _dot(h, wv_ref[...])
    ug = _dot(h, wg_ref[...])
    val = conv(uv, cwv_ref, cbv_ref, carry[0:1, :], carry[1:2, :])
    gate = conv(ug, cwg_ref, cbg_ref, carry[2:3, :], carry[3:4, :])
    carry_ref[j] = jnp.concatenate([uv[tm - 2:tm, :], ug[tm - 2:tm, :]], axis=0)
    o_ref[...] = (gate * jax.nn.sigmoid(gate) * val).astype(o_ref.dtype)


def _ffn_up(x, shift, scale, wv, wg, cwv, cwg, cbv, cbg, S, tm, tn):
    T, D = x.shape
    tm = min(tm, S)
    ncol = D_FF // tn
    return pl.pallas_call(
        functools.partial(_ffn_up_kernel, blocks_per_seq=S // tm),
        out_shape=jax.ShapeDtypeStruct((T, D_FF), BF16),
        grid=(T // tm, ncol),
        in_specs=[
            pl.BlockSpec((tm, D), lambda i, j: (i, 0)),
            pl.BlockSpec((1, 1, D), lambda i, j: ((i * tm) // S, 0, 0)),
            pl.BlockSpec((1, 1, D), lambda i, j: ((i * tm) // S, 0, 0)),
            pl.BlockSpec((D, tn), lambda i, j: (0, j)),
            pl.BlockSpec((D, tn), lambda i, j: (0, j)),
            pl.BlockSpec((3, tn), lambda i, j: (0, j)),
            pl.BlockSpec((3, tn), lambda i, j: (0, j)),
            pl.BlockSpec((1, tn), lambda i, j: (0, j)),
            pl.BlockSpec((1, tn), lambda i, j: (0, j)),
        ],
        out_specs=pl.BlockSpec((tm, tn), lambda i, j: (i, j)),
        scratch_shapes=[pltpu.VMEM((tm, D), BF16), pltpu.VMEM((ncol, 4, tn), F32)],
        compiler_params=_cparams(("arbitrary", "arbitrary")),
        name="ffn_up",
    )(x, shift, scale, wv, wg, cwv, cwg, cbv, cbg)


def _ffn_down_kernel(x_ref, a_ref, w_ref, g2_ref, fw_ref, o_ref, *, final):
    y = x_ref[...] + g2_ref[0] * _dot(a_ref[...], w_ref[...])
    if final:
        y = _rms(y) * fw_ref[...]
    o_ref[...] = y


def _ffn_down(x, act, w_down, g2, final_w, S, tm, final):
    T, D = x.shape
    tm = min(tm, S)
    return pl.pallas_call(
        functools.partial(_ffn_down_kernel, final=final),
        out_shape=jax.ShapeDtypeStruct((T, D), F32),
        grid=(T // tm,),
        in_specs=[
            pl.BlockSpec((tm, D), lambda i: (i, 0)),
            pl.BlockSpec((tm, D_FF), lambda i: (i, 0)),
            pl.BlockSpec((D_FF, D), lambda i: (0, 0)),
            pl.BlockSpec((1, 1, D), lambda i: ((i * tm) // S, 0, 0)),
            pl.BlockSpec((1, D), lambda i: (0, 0)),
        ],
        out_specs=pl.BlockSpec((tm, D), lambda i: (i, 0)),
        compiler_params=_cparams(("arbitrary",)),
        name="ffn_down",
    )(x, act, w_down, g2, final_w)


def _tiles(S):
    return dict(inproj=min(1024, S), rwkv_prep=min(512, S), hgrn=min(128, S), mla_prep=min(512, S),
                attn=min(1024, S), merge=min(512, S), ffn=min(512, S))


def _head_pad(w, per_head, lo, hi, dst, pad=MLA_PAD):
    K = w.shape[0]
    w = w.reshape(K, MLA_HEADS, per_head)[:, :, lo:hi]
    out = jnp.zeros((K, MLA_HEADS, pad), w.dtype)
    out = out.at[:, :, dst:dst + (hi - lo)].set(w)
    return out.reshape(K, MLA_HEADS * pad)


def _swap_halves(w):
    half = w.shape[-1] // 2
    return jnp.concatenate([w[..., half:], w[..., :half]], axis=-1)


def kernel(x, c, positions, ada_w, ada_b, w_in, rwkv_mu, rwkv_w0, rwkv_w2, rwkv_a0, rwkv_a2, rwkv_g2, rwkv_k_k, rwkv_k_a, rwkv_r_k, rwkv_ln_w, rwkv_ln_b, hgrn_lb, hgrn_norm_w, mla_q_norm_w, mla_w_uq, mla_kv_norm_w, mla_w_ukv, branch_proj_a, branch_proj_b, branch_proj_c, w_out, ffn_w_up, ffn_conv_w, ffn_conv_b, ffn_w_down, final_norm_w):
    B, S, D = x.shape
    L = ada_w.shape[0]
    T = B * S
    xt = x.reshape(T, D)
    pos = positions.reshape(T, 1)

    mod = _adaln(c, ada_w, ada_b).reshape(L, B, 6, 1, D)

    inv_freq = ROPE_THETA ** (-jnp.arange(0, MLA_ROPE, 2, dtype=F32) / MLA_ROPE)
    invf = jnp.zeros((1, MLA_PAD), F32).at[0, MLA_NOPE:MLA_NOPE + MLA_ROPE].set(jnp.concatenate([inv_freq, inv_freq]))

    ii = np.arange(RW_WIDTH)
    bd = jnp.asarray((ii[:, None] // RW_N == ii[None, :] // RW_N).astype(np.float32))
    cc = np.arange(RW_CHUNK)
    tri_rw = jnp.asarray((cc[:, None] >= cc[None, :]).astype(np.float32)).astype(BF16)
    bd_b = bd.astype(BF16)
    tl = _tiles(S)
    rt_hg = tl['hgrn']
    rr = np.arange(rt_hg)
    tri_hg = jnp.asarray(((rr[:, None] >= rr[None, :]) & (rr[:, None] // HG_SUB == rr[None, :] // HG_SUB)).astype(np.float32))

    o0 = RW_COLS
    o1 = o0 + HG_COLS
    o2 = o1 + MLA_Q_RANK + MLA_KV_RANK + MLA_ROPE
    for l in range(L):
        sh1, sc1, g1, sh2, sc2, g2 = (mod[l, :, j] for j in range(6))
        wl = w_in[l]
        w_rw = wl[:, 0:o0].astype(BF16)
        w_hg = wl[:, o0:o1].astype(BF16)
        w_gate = wl[:, o2:o2 + GATE_COLS].astype(BF16)
        w_kr = wl[:, o2 - MLA_ROPE:o2]
        lanes = jnp.zeros((D, MLA_PAD), F32)
        w_mla = jnp.concatenate([
            wl[:, o1:o1 + MLA_Q_RANK + MLA_KV_RANK],
            lanes.at[:, MLA_NOPE:MLA_NOPE + MLA_ROPE].set(w_kr),
            lanes.at[:, MLA_NOPE:MLA_NOPE + MLA_ROPE].set(_swap_halves(w_kr)),
        ], axis=1).astype(BF16)

        u_rw = _modmm(xt, sh1, sc1, w_rw, S, tl['inproj'], RW_COLS, "inproj_rw")
        u_hg = _modmm(xt, sh1, sc1, w_hg, S, tl['inproj'], HG_COLS, "inproj_hg")
        u_gate = _modmm(xt, sh1, sc1, w_gate, S, tl['inproj'], GATE_COLS // 2, "inproj_gate")
        u_mla = _modmm(xt, sh1, sc1, w_mla, S, tl['inproj'], MLA_SEG, "inproj_mla")

        zpad = jnp.zeros((64, RW_WIDTH), F32)
        r, k2, v, lw, kn, bn, g = _rwkv_prep(
            u_rw, rwkv_mu[l].reshape(1, -1), rwkv_w0[l].reshape(1, -1),
            jnp.concatenate([rwkv_w2[l], zpad], axis=0), rwkv_a0[l].reshape(1, -1),
            jnp.concatenate([zpad, rwkv_a2[l]], axis=0), rwkv_g2[l],
            rwkv_k_k[l].reshape(1, -1), rwkv_k_a[l].reshape(1, -1), bd, S, tl['rwkv_prep'])
        o_a = _rwkv_scan(r, k2, v, lw, kn, bn, g, rwkv_ln_w[l].reshape(1, -1), rwkv_ln_b[l].reshape(1, -1),
                         rwkv_r_k[l].reshape(1, -1), tri_rw, bd_b, B, S)

        o_b = _hgrn(u_hg, hgrn_lb, hgrn_norm_w[l].reshape(1, -1), tri_hg, l, B, S, rt_hg)

        per_q = MLA_NOPE + MLA_ROPE
        wq = _head_pad(mla_w_uq[l], per_q, 0, per_q, 0).astype(BF16)
        wq_rope = mla_w_uq[l].reshape(MLA_Q_RANK, MLA_HEADS, per_q)[:, :, MLA_NOPE:]
        wqs = _head_pad(_swap_halves(wq_rope).reshape(MLA_Q_RANK, -1), MLA_ROPE, 0, MLA_ROPE, MLA_NOPE).astype(BF16)
        per_kv = MLA_NOPE + MLA_V
        wk = _head_pad(mla_w_ukv[l], per_kv, 0, MLA_NOPE, 0).astype(BF16)
        wv = _head_pad(mla_w_ukv[l], per_kv, MLA_NOPE, per_kv, 0).astype(BF16)
        q_cat, k_cat, v_pad = _mla_prep(u_mla, pos, invf, mla_q_norm_w[l].reshape(1, -1),
                                        mla_kv_norm_w[l].reshape(1, -1), wq, wqs, wk, wv, tl['mla_prep'])
        o_c = _mla_attn(q_cat, k_cat, v_pad, B, S, tl['attn'])

        pc = jnp.zeros((MLA_HEADS, MLA_PAD, D), F32).at[:, :MLA_V, :].set(
            branch_proj_c[l].reshape(MLA_HEADS, MLA_V, D)).reshape(MLA_HEADS * MLA_PAD, D).astype(BF16)
        xt = _merge(xt, u_gate, o_a, o_b, o_c, branch_proj_a[l].astype(BF16), branch_proj_b[l].astype(BF16), pc,
                    w_out[l].astype(BF16), g1, S, tl['merge'])

        wu = ffn_w_up[l]
        cw = ffn_conv_w[l]
        cb = ffn_conv_b[l].reshape(1, -1)
        act = _ffn_up(xt, sh2, sc2, wu[:, :D_FF].astype(BF16), wu[:, D_FF:].astype(BF16),
                      cw[:, :D_FF], cw[:, D_FF:], cb[:, :D_FF], cb[:, D_FF:], S, tl['ffn'], D_FF // 2)
        xt = _ffn_down(xt, act, ffn_w_down[l].astype(BF16), g2, final_norm_w.reshape(1, -1), S, tl['ffn'], l == L - 1)

    return xt.reshape(B, S, D)
```

```python
import functools
import math

import numpy as np
import jax
import jax.numpy as jnp
from jax import lax
from jax.experimental import pallas as pl
from jax.experimental.pallas import tpu as pltpu

F32 = jnp.float32
BF16 = jnp.bfloat16
HI = lax.Precision.HIGHEST

D_MODEL = 1024
NORM_EPS = 1e-6

RW_HEADS = 8
RW_N = 64
RW_WIDTH = RW_HEADS * RW_N
RW_COLS = 1792
RW_LN_EPS = 64e-5
RW_CHUNK = 64

HG_HEADS = 4
HG_DK = 128
HG_WIDTH = HG_HEADS * HG_DK
HG_COLS = 2048
HG_SUB = 16
F_FLOOR = 1e-30

MLA_HEADS = 8
MLA_Q_RANK = 256
MLA_KV_RANK = 128
MLA_NOPE = 64
MLA_ROPE = 32
MLA_V = 64
MLA_PAD = 128
MLA_SEG = 640
MLA_SCALE = (MLA_NOPE + MLA_ROPE) ** -0.5
MLA_EXP_SCALE = MLA_SCALE * math.log2(math.e)
ROPE_THETA = 10000.0
NEG_BIG = -1e30

D_FF = 2816
GATE_COLS = 3 * D_MODEL

VMEM_LIMIT = 56 * 1024 * 1024


def _cparams(sem):
    return pltpu.CompilerParams(dimension_semantics=sem, vmem_limit_bytes=VMEM_LIMIT)


def _dot(a, b, precision=None):
    return jnp.dot(a, b, precision=precision, preferred_element_type=F32)


def _dot_nt(a, b, precision=None):
    return lax.dot_general(a, b, (((1,), (1,)), ((), ())), precision=precision, preferred_element_type=F32)


def _dot_tn(a, b, precision=None):
    return lax.dot_general(a, b, (((0,), (0,)), ((), ())), precision=precision, preferred_element_type=F32)


def _rms(x):
    return x * lax.rsqrt(jnp.mean(x * x, axis=-1, keepdims=True) + NORM_EPS)


def _adaln_kernel(c_ref, w_ref, b_ref, o_ref):
    c = c_ref[...]
    cond = c * jax.nn.sigmoid(c)
    o_ref[0] = _dot(cond, w_ref[0], HI) + b_ref[0]


def _adaln(c, ada_w, ada_b):
    L, D, N = ada_w.shape
    B = c.shape[0]
    tn = 1536
    return pl.pallas_call(
        _adaln_kernel,
        out_shape=jax.ShapeDtypeStruct((L, B, N), F32),
        grid=(L, N // tn),
        in_specs=[
            pl.BlockSpec((B, D), lambda l, j: (0, 0)),
            pl.BlockSpec((1, D, tn), lambda l, j: (l, 0, j)),
            pl.BlockSpec((1, 1, tn), lambda l, j: (l, 0, j)),
        ],
        out_specs=pl.BlockSpec((1, B, tn), lambda l, j: (l, 0, j)),
        compiler_params=_cparams(("arbitrary", "arbitrary")),
        name="adaln",
    )(c, ada_w, ada_b.reshape(L, 1, N))


def _modmm_kernel(x_ref, sh_ref, sc_ref, w_ref, o_ref, h_ref):
    @pl.when(pl.program_id(1) == 0)
    def _():
        h = _rms(x_ref[...]) * (1.0 + sc_ref[0]) + sh_ref[0]
        h_ref[...] = h.astype(BF16)

    o_ref[...] = _dot(h_ref[...], w_ref[...]).astype(o_ref.dtype)


def _modmm(x, shift, scale, w, S, tm, tn, name):
    T, D = x.shape
    N = w.shape[1]
    tm = min(tm, S)
    return pl.pallas_call(
        _modmm_kernel,
        out_shape=jax.ShapeDtypeStruct((T, N), F32),
        grid=(T // tm, N // tn),
        in_specs=[
            pl.BlockSpec((tm, D), lambda i, j: (i, 0)),
            pl.BlockSpec((1, 1, D), lambda i, j: ((i * tm) // S, 0, 0)),
            pl.BlockSpec((1, 1, D), lambda i, j: ((i * tm) // S, 0, 0)),
            pl.BlockSpec((D, tn), lambda i, j: (0, j)),
        ],
        out_specs=pl.BlockSpec((tm, tn), lambda i, j: (i, j)),
        scratch_shapes=[pltpu.VMEM((tm, D), BF16)],
        compiler_params=_cparams(("arbitrary", "arbitrary")),
        name=name,
    )(x, shift, scale, w)


def _rwkv_prep_kernel(u_ref, mu_ref, w0_ref, w2_ref, a0_ref, a2_ref, g2_ref, kk_ref, ka_ref, bd_ref,
                      r_out, k_out, v_out, lw_out, kn_out, bn_out, g_out, carry_ref, *, blocks_per_seq):
    @pl.when(pl.program_id(0) % blocks_per_seq == 0)
    def _():
        carry_ref[...] = jnp.zeros_like(carry_ref)

    u = u_ref[...]
    tm = u.shape[0]
    row = lax.broadcasted_iota(jnp.int32, u.shape, 0)
    prev = jnp.where(row == 0, carry_ref[...], pltpu.roll(u, 1, 0))
    carry_ref[...] = u[tm - 1:tm, :]
    xm = u + (prev - u) * mu_ref[...]
    W = RW_WIDTH
    r = xm[:, 0:W]
    k = xm[:, W:2 * W]
    v = xm[:, 2 * W:3 * W]
    la = xm[:, 3 * W:3 * W + 128]
    gl = xm[:, 3 * W + 128:3 * W + 256]
    w = -jax.nn.softplus(-(w0_ref[...] + _dot(jnp.tanh(la).astype(BF16), w2_ref[...]))) - 0.5
    lw = -jnp.exp(w)
    a = jax.nn.sigmoid(a0_ref[...] + _dot(la.astype(BF16), a2_ref[...]))
    g = _dot(jax.nn.sigmoid(gl).astype(BF16), g2_ref[...])
    kk = k * kk_ref[...]
    k2hi = (kk * kk).astype(BF16)
    k2lo = (kk * kk - k2hi.astype(F32)).astype(BF16)
    bd = bd_ref[...]
    ss = _dot(k2hi, bd) + _dot(k2lo, bd)
    kn = kk / jnp.maximum(jnp.sqrt(ss), 1e-12)
    k2 = k * (1.0 + (a - 1.0) * ka_ref[...])
    r_out[...] = r
    k_out[...] = k2
    v_out[...] = v
    lw_out[...] = lw
    kn_out[...] = kn
    bn_out[...] = kn * a
    g_out[...] = g


def _rwkv_prep(u_rw, mu, w0, w2p, a0, a2p, g2, k_k, k_a, bd, S, tm):
    T = u_rw.shape[0]
    tm = min(tm, S)
    W = RW_WIDTH
    full = lambda shape: pl.BlockSpec(shape, lambda i: (0,) * len(shape))
    row_spec = pl.BlockSpec((tm, W), lambda i: (i, 0))
    return pl.pallas_call(
        functools.partial(_rwkv_prep_kernel, blocks_per_seq=S // tm),
        out_shape=[jax.ShapeDtypeStruct((T, W), F32)] * 7,
        grid=(T // tm,),
        in_specs=[
            pl.BlockSpec((tm, RW_COLS), lambda i: (i, 0)),
            full((1, RW_COLS)), full((1, W)), full((128, W)), full((1, W)), full((128, W)), full((128, W)),
            full((1, W)), full((1, W)), full((W, W)),
        ],
        out_specs=[row_spec] * 7,
        scratch_shapes=[pltpu.VMEM((1, RW_COLS), F32)],
        compiler_params=_cparams(("arbitrary",)),
        name="rwkv_prep",
    )(u_rw, mu, w0, w2p, a0, a2p, g2, k_k, k_a, bd)


def _rwkv_scan_kernel(r_ref, k_ref, v_ref, lw_ref, kn_ref, bn_ref, g_ref, lnw_ref, lnb_ref, rk_ref, tri_ref, bd_ref,
                      o_ref, s_ref, y_ref):
    @pl.when(pl.program_id(0) == 0)
    def _():
        s_ref[...] = jnp.zeros_like(s_ref)

    C = RW_CHUNK
    N = RW_N
    B = r_ref.shape[0]
    tri = tri_ref[...]
    bd = bd_ref[...]
    row = lax.broadcasted_iota(jnp.int32, (C, 2 * C), 0)
    col = lax.broadcasted_iota(jnp.int32, (C, 2 * C), 1) % C
    strict = row > col
    incl = row >= col
    eye = (lax.broadcasted_iota(jnp.int32, (C, C), 0) == lax.broadcasted_iota(jnp.int32, (C, C), 1)).astype(F32)

    wide = []
    for b in range(B):
        r = r_ref[b]
        k = k_ref[b]
        v = v_ref[b]
        lw = lw_ref[b]
        kn = kn_ref[b]
        bn = bn_ref[b]
        lw1 = lw.astype(BF16)
        rem = lw - lw1.astype(F32)
        lw2 = rem.astype(BF16)
        lw3 = (rem - lw2.astype(F32)).astype(BF16)
        c = _dot(tri, lw1) + _dot(tri, lw2) + _dot(tri, lw3)
        c_end = c[C - 1:C, :]
        e_neg = jnp.exp(-c)
        e_end = jnp.exp(c_end - c)
        wide.append(dict(
            a=(-kn * jnp.exp(c - lw)).astype(BF16),
            b=(bn * e_neg).astype(BF16),
            k=(k * e_neg).astype(BF16),
            r=(r * jnp.exp(c)).astype(BF16),
            bh=(bn * e_end).astype(BF16),
            kh=(k * e_end).astype(BF16),
            v=v.astype(BF16),
            g_end=jnp.exp(c_end)))

    chains = [(b, h) for b in range(B) for h in range(RW_HEADS)]
    sl = lambda h: slice(h * N, (h + 1) * N)
    m4 = [_dot_nt(jnp.concatenate([wide[b]['a'][:, sl(h)], wide[b]['r'][:, sl(h)]], axis=0),
                  jnp.concatenate([wide[b]['b'][:, sl(h)], wide[b]['k'][:, sl(h)]], axis=0)) for b, h in chains]
    abk = [jnp.where(strict, m[0:C, :], 0.0) for m in m4]
    rbk = [jnp.where(incl, m[C:2 * C, :], 0.0).astype(BF16) for m in m4]
    npow = [m[:, 0:C] for m in abk]
    tinv = [eye + n for n in npow]
    for _ in range(5):
        nb = [n.astype(BF16) for n in npow]
        npow = [_dot(n, n) for n in nb]
        tinv = [t + _dot(t.astype(BF16), n.astype(BF16)) for t, n in zip(tinv, npow)]
    s0 = [s_ref[b * RW_HEADS + h] for b, h in chains]
    s0b = [s.astype(BF16) for s in s0]
    x = [_dot_nt(wide[b]['a'][:, sl(h)], s) for (b, h), s in zip(chains, s0b)]
    x = [xi + _dot(m[:, C:2 * C].astype(BF16), wide[b]['v'][:, sl(h)]) for (b, h), xi, m in zip(chains, x, abk)]
    u = [_dot(t.astype(BF16), xi.astype(BF16)) for t, xi in zip(tinv, x)]
    uv = [jnp.concatenate([ui.astype(BF16), wide[b]['v'][:, sl(h)]], axis=0) for (b, h), ui in zip(chains, u)]
    y = [_dot_nt(wide[b]['r'][:, sl(h)], s) for (b, h), s in zip(chains, s0b)]
    y = [yi + _dot(m, w) for yi, m, w in zip(y, rbk, uv)]
    snew = [_dot_tn(w, jnp.concatenate([wide[b]['bh'][:, sl(h)], wide[b]['kh'][:, sl(h)]], axis=0))
            for (b, h), w in zip(chains, uv)]
    for (b, h), s, sn, yi in zip(chains, s0, snew, y):
        s_ref[b * RW_HEADS + h] = s * wide[b]['g_end'][:, sl(h)] + sn
        y_ref[b, :, sl(h)] = yi

    for b in range(B):
        r = r_ref[b]
        k = k_ref[b]
        v = v_ref[b]
        y = y_ref[b]
        mean = _dot(y.astype(BF16), bd) * (1.0 / N)
        d = y - mean
        var = _dot((d * d).astype(BF16), bd) * (1.0 / N)
        yn = d * lax.rsqrt(var + RW_LN_EPS) * lnw_ref[...] + lnb_ref[...]
        bonus = _dot((r * k * rk_ref[...]).astype(BF16), bd)
        o_ref[b] = (yn + bonus * v) * g_ref[b]


def _rwkv_scan(r, k, v, lw, kn, bn, g, ln_w, ln_b, r_k, tri, bd, B, S):
    T, W = r.shape
    C = RW_CHUNK
    blk = pl.BlockSpec((B, C, W), lambda c: (0, c, 0))
    full = lambda shape: pl.BlockSpec(shape, lambda c: (0,) * len(shape))
    seq = lambda t: t.reshape(B, S, W)
    out = pl.pallas_call(
        _rwkv_scan_kernel,
        out_shape=jax.ShapeDtypeStruct((B, S, W), F32),
        grid=(S // C,),
        in_specs=[blk] * 7 + [full((1, W))] * 3 + [full((C, C)), full((W, W))],
        out_specs=blk,
        scratch_shapes=[pltpu.VMEM((B * RW_HEADS, RW_N, RW_N), F32), pltpu.VMEM((B, C, W), F32)],
        compiler_params=_cparams(("arbitrary",)),
        name="rwkv_scan",
    )(seq(r), seq(k), seq(v), seq(lw), seq(kn), seq(bn), seq(g), ln_w, ln_b, r_k, tri, bd)
    return out.reshape(T, W)


def _hgrn_kernel(q_ref, fz_ref, i_ref, og_ref, lb_ref, nw_ref, tri_ref, o_ref, st_ref, *, layer):
    @pl.when(pl.program_id(2) == 0)
    def _():
        st_ref[...] = jnp.zeros_like(st_ref)

    lbraw = lb_ref[...]
    e = jnp.exp(lbraw - jnp.max(lbraw, axis=0, keepdims=True))
    p = e / jnp.sum(e, axis=0, keepdims=True)
    lb = jnp.zeros((1, p.shape[1]), F32)
    for l in range(1, layer + 1):
        lb = lb + p[l:l + 1, :]
    fz = fz_ref[...]
    f = lb + (1.0 - lb) * jax.nn.sigmoid(fz)
    g = jnp.log(jnp.maximum(f, F_FLOOR))
    key = (1.0 - lb) * jax.nn.sigmoid(-fz)
    bcum = _dot(tri_ref[...], g, HI)
    q = q_ref[...]
    v = i_ref[...]
    qe = q * jnp.exp(bcum)
    rt = q.shape[0]
    n = HG_SUB
    rowi = lax.broadcasted_iota(jnp.int32, (n, HG_DK), 0)
    chunks = [slice(c * n, (c + 1) * n) for c in range(rt // n)]
    b_last = [bcum[rs][n - 1:n, :] for rs in chunks]
    inc = [_dot_tn(v[rs].astype(BF16), (key[rs] * jnp.exp(bl - bcum[rs])).astype(BF16))
           for rs, bl in zip(chunks, b_last)]
    intra = []
    for rs in chunks:
        bc, qc, kc, vc = bcum[rs], q[rs], key[rs], v[rs]
        o = jnp.zeros((n, HG_DK), F32)
        for s in range(n):
            dec = jnp.exp(jnp.where(rowi >= s, bc - bc[s:s + 1, :], NEG_BIG))
            o = o + jnp.sum(qc * dec * kc[s:s + 1, :], axis=-1, keepdims=True) * vc[s:s + 1, :]
        intra.append(o)
    st = st_ref[...]
    starts = []
    for bl, d in zip(b_last, inc):
        starts.append(st)
        st = st * jnp.exp(bl) + d
    st_ref[...] = st
    inter = [_dot_nt(qe[rs].astype(BF16), s0.astype(BF16)) for rs, s0 in zip(chunks, starts)]
    o = jnp.concatenate([a + b for a, b in zip(inter, intra)], axis=0)
    og = og_ref[...]
    o_ref[...] = _rms(o) * nw_ref[...] * (og * jax.nn.sigmoid(og))


def _hgrn(u_hg, lb_raw, norm_w, tri, layer, B, S, rt):
    T = u_hg.shape[0]
    rt = min(rt, S)
    nR = S // rt
    L = lb_raw.shape[0]
    H = HG_HEADS

    def col(off):
        return pl.BlockSpec((rt, HG_DK), lambda b, h, i: (b * nR + i, off + h))

    return pl.pallas_call(
        functools.partial(_hgrn_kernel, layer=layer),
        out_shape=jax.ShapeDtypeStruct((T, HG_WIDTH), F32),
        grid=(B, H, nR),
        in_specs=[
            col(0), col(H), col(2 * H), col(3 * H),
            pl.BlockSpec((L, HG_DK), lambda b, h, i: (0, h)),
            pl.BlockSpec((1, HG_DK), lambda b, h, i: (0, 0)),
            pl.BlockSpec((rt, rt), lambda b, h, i: (0, 0)),
        ],
        out_specs=pl.BlockSpec((rt, HG_DK), lambda b, h, i: (b * nR + i, h)),
        scratch_shapes=[pltpu.VMEM((HG_DK, HG_DK), F32)],
        compiler_params=_cparams(("arbitrary", "arbitrary", "arbitrary")),
        name="hgrn",
    )(u_hg, u_hg, u_hg, u_hg, lb_raw, norm_w, tri)


def _mla_prep_kernel(u_ref, pos_ref, invf_ref, qnw_ref, kvnw_ref, wq_ref, wqs_ref, wk_ref, wvt_ref,
                     q_out, k_out, vt_out):
    u = u_ref[...]
    cq = u[:, 0:MLA_Q_RANK]
    ckv = u[:, MLA_Q_RANK:MLA_Q_RANK + MLA_KV_RANK]
    kr = u[:, 384:512]
    krs = u[:, 512:640]
    ang = pos_ref[...].astype(F32) * invf_ref[...]
    lane = lax.broadcasted_iota(jnp.int32, ang.shape, 1)
    cosv = jnp.cos(ang)
    sinv = jnp.sin(ang)
    cos_t = jnp.where(lane < MLA_NOPE, 1.0, jnp.where(lane < MLA_NOPE + MLA_ROPE, cosv, 0.0))
    sin_t = jnp.where(lane < MLA_NOPE, 0.0,
                      jnp.where(lane < MLA_NOPE + MLA_ROPE // 2, -sinv,
                                jnp.where(lane < MLA_NOPE + MLA_ROPE, sinv, 0.0)))
    cqn = (_rms(cq) * qnw_ref[...]).astype(BF16)
    kvn = (_rms(ckv) * kvnw_ref[...]).astype(BF16)
    qm = _dot(cqn, wq_ref[...])
    qs = _dot(cqn, wqs_ref[...])
    kn = _dot(kvn, wk_ref[...])
    vt = _dot_nt(wvt_ref[...], kvn)
    k_rot = kr * cos_t + krs * sin_t
    for h in range(MLA_HEADS):
        sl = slice(h * MLA_PAD, (h + 1) * MLA_PAD)
        q_out[:, sl] = ((qm[:, sl] * cos_t + qs[:, sl] * sin_t) * MLA_EXP_SCALE).astype(BF16)
        k_out[:, sl] = (kn[:, sl] + k_rot).astype(BF16)
    vrow = lax.broadcasted_iota(jnp.int32, vt.shape, 0) % MLA_PAD
    vt_out[...] = jnp.where(vrow == MLA_V, 1.0, vt).astype(BF16)


def _mla_prep(u_mla, pos, invf, qnw, kvnw, wq, wqs, wk, wvt, tm):
    T = u_mla.shape[0]
    tm = min(tm, T)
    W = MLA_HEADS * MLA_PAD
    full = lambda shape: pl.BlockSpec(shape, lambda i: (0,) * len(shape))
    out_spec = pl.BlockSpec((tm, W), lambda i: (i, 0))
    return pl.pallas_call(
        _mla_prep_kernel,
        out_shape=[jax.ShapeDtypeStruct((T, W), BF16)] * 2 + [jax.ShapeDtypeStruct((W, T), BF16)],
        grid=(T // tm,),
        in_specs=[
            pl.BlockSpec((tm, MLA_SEG), lambda i: (i, 0)),
            pl.BlockSpec((tm, 1), lambda i: (i, 0)),
            full((1, MLA_PAD)), full((1, MLA_Q_RANK)), full((1, MLA_KV_RANK)),
            full((MLA_Q_RANK, W)), full((MLA_Q_RANK, W)), full((MLA_KV_RANK, W)), full((W, MLA_KV_RANK)),
        ],
        out_specs=[out_spec] * 2 + [pl.BlockSpec((W, tm), lambda i: (0, i))],
        compiler_params=_cparams(("arbitrary",)),
        name="mla_prep",
    )(u_mla, pos, invf, qnw, kvnw, wq, wqs, wk, wvt)


def _attn_kernel(qi_ref, ki_ref, q_ref, k_ref, vt_ref, o_ref, m_ref, acc_ref, *, groups):
    t = pl.program_id(2)
    qi = qi_ref[t]
    ki = ki_ref[t]
    gw = q_ref.shape[0] // groups

    @pl.when(ki == 0)
    def _():
        m_ref[...] = jnp.full_like(m_ref, NEG_BIG)
        acc_ref[...] = jnp.zeros_like(acc_ref)

    def step(diag):
        kv_len = lambda g: (g + 1) * gw if diag else k_ref.shape[0]
        st = [_dot_nt(k_ref[0:kv_len(g), :], q_ref[g * gw:(g + 1) * gw, :]) for g in range(groups)]
        for g in range(groups):
            s = st[g]
            if diag:
                row = lax.broadcasted_iota(jnp.int32, s.shape, 0)
                col = lax.broadcasted_iota(jnp.int32, s.shape, 1) + g * gw
                s = jnp.where(row <= col, s, NEG_BIG)
            cs = slice(g * gw, (g + 1) * gw)
            m_prev = m_ref[:, cs]
            m_new = jnp.maximum(m_prev, jnp.max(s, axis=0, keepdims=True))
            p = jnp.exp2(s - m_new).astype(BF16)
            acc_ref[:, cs] = jnp.exp2(m_prev - m_new) * acc_ref[:, cs] + _dot(vt_ref[:, 0:kv_len(g)], p)
            m_ref[:, cs] = m_new

    @pl.when(ki < qi)
    def _():
        step(False)

    @pl.when(ki == qi)
    def _():
        step(True)
        acc = acc_ref[...]
        o_ref[...] = (acc / acc[MLA_V:MLA_V + 1, :]).T.astype(o_ref.dtype)


def _mla_attn(q, k, vt, B, S, blk):
    T, W = q.shape
    blk = min(blk, S)
    nq = S // blk
    groups = max(1, min(8, blk // 128))
    qi = np.array([i for i in range(nq) for _ in range(i + 1)], np.int32)
    ki = np.array([j for i in range(nq) for j in range(i + 1)], np.int32)
    grid_spec = pltpu.PrefetchScalarGridSpec(
        num_scalar_prefetch=2,
        grid=(B, MLA_HEADS, len(qi)),
        in_specs=[
            pl.BlockSpec((blk, MLA_PAD), lambda b, h, t, qi, ki: (b * nq + qi[t], h)),
            pl.BlockSpec((blk, MLA_PAD), lambda b, h, t, qi, ki: (b * nq + ki[t], h)),
            pl.BlockSpec((MLA_PAD, blk), lambda b, h, t, qi, ki: (h, b * nq + ki[t])),
        ],
        out_specs=pl.BlockSpec((blk, MLA_PAD), lambda b, h, t, qi, ki: (b * nq + qi[t], h)),
        scratch_shapes=[pltpu.VMEM((1, blk), F32), pltpu.VMEM((MLA_PAD, blk), F32)],
    )
    return pl.pallas_call(
        functools.partial(_attn_kernel, groups=groups),
        out_shape=jax.ShapeDtypeStruct((T, W), BF16),
        grid_spec=grid_spec,
        compiler_params=_cparams(("arbitrary", "arbitrary", "arbitrary")),
        name="mla_attn",
    )(jnp.asarray(qi), jnp.asarray(ki), q, k, vt)


def _merge_kernel(x_ref, gt_ref, oa_ref, ob_ref, oc_ref, pa_ref, pb_ref, pc_ref, wo_ref, g1_ref, o_ref):
    D = D_MODEL
    gt = gt_ref[...]
    y = jax.nn.sigmoid(gt[:, 0:D]) * _dot(oa_ref[...].astype(BF16), pa_ref[...])
    y = y + jax.nn.sigmoid(gt[:, D:2 * D]) * _dot(ob_ref[...].astype(BF16), pb_ref[...])
    y = y + jax.nn.sigmoid(gt[:, 2 * D:3 * D]) * _dot(oc_ref[...], pc_ref[...])
    o_ref[...] = x_ref[...] + g1_ref[0] * _dot(y.astype(BF16), wo_ref[...])


def _merge(x, u_gate, o_a, o_b, o_c, pa, pb, pc, wo, g1, S, tm):
    T, D = x.shape
    tm = min(tm, S)
    rows = lambda w: pl.BlockSpec((tm, w), lambda i: (i, 0))
    full = lambda shape: pl.BlockSpec(shape, lambda i: (0,) * len(shape))
    return pl.pallas_call(
        _merge_kernel,
        out_shape=jax.ShapeDtypeStruct((T, D), F32),
        grid=(T // tm,),
        in_specs=[
            rows(D), rows(GATE_COLS), rows(RW_WIDTH), rows(HG_WIDTH), rows(MLA_HEADS * MLA_PAD),
            full(pa.shape), full(pb.shape), full(pc.shape), full(wo.shape),
            pl.BlockSpec((1, 1, D), lambda i: ((i * tm) // S, 0, 0)),
        ],
        out_specs=rows(D),
        compiler_params=_cparams(("arbitrary",)),
        name="merge",
    )(x, u_gate, o_a, o_b, o_c, pa, pb, pc, wo, g1)


def _ffn_up_kernel(x_ref, sh_ref, sc_ref, wv_ref, wg_ref, cwv_ref, cwg_ref, cbv_ref, cbg_ref, o_ref,
                   h_ref, carry_ref, *, blocks_per_seq):
    j = pl.program_id(1)

    @pl.when(j == 0)
    def _():
        h = _rms(x_ref[...]) * (1.0 + sc_ref[0]) + sh_ref[0]
        h_ref[...] = h.astype(BF16)

    @pl.when(pl.program_id(0) % blocks_per_seq == 0)
    def _():
        carry_ref[j] = jnp.zeros(carry_ref.shape[1:], F32)

    h = h_ref[...]
    tm = h.shape[0]
    carry = carry_ref[j]

    def conv(u, cw_ref, cb_ref, c2, c1):
        row = lax.broadcasted_iota(jnp.int32, (8, u.shape[1]), 0)
        r1 = pltpu.roll(u, 1, 0)
        r2 = pltpu.roll(u, 2, 0)
        u1 = jnp.concatenate([jnp.where(row == 0, c1, r1[0:8, :]), r1[8:, :]], axis=0)
        u2 = jnp.concatenate([jnp.where(row == 0, c2, jnp.where(row == 1, c1, r2[0:8, :])), r2[8:, :]], axis=0)
        cw = cw_ref[...]
        return cw[0:1, :] * u2 + cw[1:2, :] * u1 + cw[2:3, :] * u + cb_ref[...]

    uv = _dot(h, wv_ref[...])
    ug = _dot(h, wg_ref[...])
    val = conv(uv, cwv_ref, cbv_ref, carry[0:1, :], carry[1:2, :])
    gate = conv(ug, cwg_ref, cbg_ref, carry[2:3, :], carry[3:4, :])
    carry_ref[j] = jnp.concatenate([uv[tm - 2:tm, :], ug[tm - 2:tm, :]], axis=0)
    o_ref[...] = (gate * jax.nn.sigmoid(gate) * val).astype(o_ref.dtype)


def _ffn_up(x, shift, scale, wv, wg, cwv, cwg, cbv, cbg, S, tm, tn):
    T, D = x.shape
    tm = min(tm, S)
    ncol = D_FF // tn
    return pl.pallas_call(
        functools.partial(_ffn_up_kernel, blocks_per_seq=S // tm),
        out_shape=jax.ShapeDtypeStruct((T, D_FF), BF16),
        grid=(T // tm, ncol),
        in_specs=[
            pl.BlockSpec((tm, D), lambda i, j: (i, 0)),
            pl.BlockSpec((1, 1, D), lambda i, j: ((i * tm) // S, 0, 0)),
            pl.BlockSpec((1, 1, D), lambda i, j: ((i * tm) // S, 0, 0)),
            pl.BlockSpec((D, tn), lambda i, j: (0, j)),
            pl.BlockSpec((D, tn), lambda i, j: (0, j)),
            pl.BlockSpec((3, tn), lambda i, j: (0, j)),
            pl.BlockSpec((3, tn), lambda i, j: (0, j)),
            pl.BlockSpec((1, tn), lambda i, j: (0, j)),
            pl.BlockSpec((1, tn), lambda i, j: (0, j)),
        ],
        out_specs=pl.BlockSpec((tm, tn), lambda i, j: (i, j)),
        scratch_shapes=[pltpu.VMEM((tm, D), BF16), pltpu.VMEM((ncol, 4, tn), F32)],
        compiler_params=_cparams(("arbitrary", "arbitrary")),
        name="ffn_up",
    )(x, shift, scale, wv, wg, cwv, cwg, cbv, cbg)


def _ffn_down_kernel(x_ref, a_ref, w_ref, g2_ref, fw_ref, o_ref, *, final):
    y = x_ref[...] + g2_ref[0] * _dot(a_ref[...], w_ref[...])
    if final:
        y = _rms(y) * fw_ref[...]
    o_ref[...] = y


def _ffn_down(x, act, w_down, g2, final_w, S, tm, final):
    T, D = x.shape
    tm = min(tm, S)
    return pl.pallas_call(
        functools.partial(_ffn_down_kernel, final=final),
        out_shape=jax.ShapeDtypeStruct((T, D), F32),
        grid=(T // tm,),
        in_specs=[
            pl.BlockSpec((tm, D), lambda i: (i, 0)),
            pl.BlockSpec((tm, D_FF), lambda i: (i, 0)),
            pl.BlockSpec((D_FF, D), lambda i: (0, 0)),
            pl.BlockSpec((1, 1, D), lambda i: ((i * tm) // S, 0, 0)),
            pl.BlockSpec((1, D), lambda i: (0, 0)),
        ],
        out_specs=pl.BlockSpec((tm, D), lambda i: (i, 0)),
        compiler_params=_cparams(("arbitrary",)),
        name="ffn_down",
    )(x, act, w_down, g2, final_w)


def _tiles(S):
    return dict(inproj=min(1024, S), rwkv_prep=min(512, S), hgrn=min(256, S), mla_prep=min(512, S),
                attn=min(2048, S), merge=min(512, S), ffn=min(512, S))


def _head_pad(w, per_head, lo, hi, dst, pad=MLA_PAD):
    K = w.shape[0]
    w = w.reshape(K, MLA_HEADS, per_head)[:, :, lo:hi]
    out = jnp.zeros((K, MLA_HEADS, pad), w.dtype)
    out = out.at[:, :, dst:dst + (hi - lo)].set(w)
    return out.reshape(K, MLA_HEADS * pad)


def _swap_halves(w):
    half = w.shape[-1] // 2
    return jnp.concatenate([w[..., half:], w[..., :half]], axis=-1)


def kernel(x, c, positions, ada_w, ada_b, w_in, rwkv_mu, rwkv_w0, rwkv_w2, rwkv_a0, rwkv_a2, rwkv_g2, rwkv_k_k, rwkv_k_a, rwkv_r_k, rwkv_ln_w, rwkv_ln_b, hgrn_lb, hgrn_norm_w, mla_q_norm_w, mla_w_uq, mla_kv_norm_w, mla_w_ukv, branch_proj_a, branch_proj_b, branch_proj_c, w_out, ffn_w_up, ffn_conv_w, ffn_conv_b, ffn_w_down, final_norm_w):
    B, S, D = x.shape
    L = ada_w.shape[0]
    T = B * S
    xt = x.reshape(T, D)
    pos = positions.reshape(T, 1)

    mod = _adaln(c, ada_w, ada_b).reshape(L, B, 6, 1, D)

    inv_freq = ROPE_THETA ** (-jnp.arange(0, MLA_ROPE, 2, dtype=F32) / MLA_ROPE)
    invf = jnp.zeros((1, MLA_PAD), F32).at[0, MLA_NOPE:MLA_NOPE + MLA_ROPE].set(jnp.concatenate([inv_freq, inv_freq]))

    ii = np.arange(RW_WIDTH)
    bd = jnp.asarray((ii[:, None] // RW_N == ii[None, :] // RW_N).astype(np.float32))
    cc = np.arange(RW_CHUNK)
    tri_rw = jnp.asarray((cc[:, None] >= cc[None, :]).astype(np.float32)).astype(BF16)
    bd_b = bd.astype(BF16)
    tl = _tiles(S)
    rt_hg = tl['hgrn']
    rr = np.arange(rt_hg)
    tri_hg = jnp.asarray(((rr[:, None] >= rr[None, :]) & (rr[:, None] // HG_SUB == rr[None, :] // HG_SUB)).astype(np.float32))

    o0 = RW_COLS
    o1 = o0 + HG_COLS
    o2 = o1 + MLA_Q_RANK + MLA_KV_RANK + MLA_ROPE
    for l in range(L):
        sh1, sc1, g1, sh2, sc2, g2 = (mod[l, :, j] for j in range(6))
        wl = w_in[l]
        w_rw = wl[:, 0:o0].astype(BF16)
        w_hg = wl[:, o0:o1].astype(BF16)
        w_gate = wl[:, o2:o2 + GATE_COLS].astype(BF16)
        w_kr = wl[:, o2 - MLA_ROPE:o2]
        lanes = jnp.zeros((D, MLA_PAD), F32)
        w_mla = jnp.concatenate([
            wl[:, o1:o1 + MLA_Q_RANK + MLA_KV_RANK],
            lanes.at[:, MLA_NOPE:MLA_NOPE + MLA_ROPE].set(w_kr),
            lanes.at[:, MLA_NOPE:MLA_NOPE + MLA_ROPE].set(_swap_halves(w_kr)),
        ], axis=1).astype(BF16)

        u_rw = _modmm(xt, sh1, sc1, w_rw, S, tl['inproj'], RW_COLS, "inproj_rw")
        u_hg = _modmm(xt, sh1, sc1, w_hg, S, tl['inproj'], HG_COLS, "inproj_hg")
        u_gate = _modmm(xt, sh1, sc1, w_gate, S, tl['inproj'], GATE_COLS // 2, "inproj_gate")
        u_mla = _modmm(xt, sh1, sc1, w_mla, S, tl['inproj'], MLA_SEG, "inproj_mla")

        zpad = jnp.zeros((64, RW_WIDTH), F32)
        r, k2, v, lw, kn, bn, g = _rwkv_prep(
            u_rw, rwkv_mu[l].reshape(1, -1), rwkv_w0[l].reshape(1, -1),
            jnp.concatenate([rwkv_w2[l], zpad], axis=0).astype(BF16), rwkv_a0[l].reshape(1, -1),
            jnp.concatenate([zpad, rwkv_a2[l]], axis=0).astype(BF16), rwkv_g2[l].astype(BF16),
            rwkv_k_k[l].reshape(1, -1), rwkv_k_a[l].reshape(1, -1), bd_b, S, tl['rwkv_prep'])
        o_a = _rwkv_scan(r, k2, v, lw, kn, bn, g, rwkv_ln_w[l].reshape(1, -1), rwkv_ln_b[l].reshape(1, -1),
                         rwkv_r_k[l].reshape(1, -1), tri_rw, bd_b, B, S)

        o_b = _hgrn(u_hg, hgrn_lb, hgrn_norm_w[l].reshape(1, -1), tri_hg, l, B, S, rt_hg)

        per_q = MLA_NOPE + MLA_ROPE
        wq = _head_pad(mla_w_uq[l], per_q, 0, per_q, 0).astype(BF16)
        wq_rope = mla_w_uq[l].reshape(MLA_Q_RANK, MLA_HEADS, per_q)[:, :, MLA_NOPE:]
        wqs = _head_pad(_swap_halves(wq_rope).reshape(MLA_Q_RANK, -1), MLA_ROPE, 0, MLA_ROPE, MLA_NOPE).astype(BF16)
        per_kv = MLA_NOPE + MLA_V
        wk = _head_pad(mla_w_ukv[l], per_kv, 0, MLA_NOPE, 0).astype(BF16)
        wvt = _head_pad(mla_w_ukv[l], per_kv, MLA_NOPE, per_kv, 0).T.astype(BF16)
        q_cat, k_cat, v_t = _mla_prep(u_mla, pos, invf, mla_q_norm_w[l].reshape(1, -1),
                                      mla_kv_norm_w[l].reshape(1, -1), wq, wqs, wk, wvt, tl['mla_prep'])
        o_c = _mla_attn(q_cat, k_cat, v_t, B, S, tl['attn'])

        pc = jnp.zeros((MLA_HEADS, MLA_PAD, D), F32).at[:, :MLA_V, :].set(
            branch_proj_c[l].reshape(MLA_HEADS, MLA_V, D)).reshape(MLA_HEADS * MLA_PAD, D).astype(BF16)
        xt = _merge(xt, u_gate, o_a, o_b, o_c, branch_proj_a[l].astype(BF16), branch_proj_b[l].astype(BF16), pc,
                    w_out[l].astype(BF16), g1, S, tl['merge'])

        wu = ffn_w_up[l]
        cw = ffn_conv_w[l]
        cb = ffn_conv_b[l].reshape(1, -1)
        act = _ffn_up(xt, sh2, sc2, wu[:, :D_FF].astype(BF16), wu[:, D_FF:].astype(BF16),
                      cw[:, :D_FF], cw[:, D_FF:], cb[:, :D_FF], cb[:, D_FF:], S, tl['ffn'], D_FF // 2)
        xt = _ffn_down(xt, act, ffn_w_down[l].astype(BF16), g2, final_norm_w.reshape(1, -1), S, tl['ffn'], l == L - 1)

    return xt.reshape(B, S, D)
```

```python
import functools
import math

import numpy as np
import jax
import jax.numpy as jnp
from jax import lax
from jax.experimental import pallas as pl
from jax.experimental.pallas import tpu as pltpu

F32 = jnp.float32
BF16 = jnp.bfloat16
HI = lax.Precision.HIGHEST

D_MODEL = 1024
NORM_EPS = 1e-6

RW_HEADS = 8
RW_N = 64
RW_WIDTH = RW_HEADS * RW_N
RW_COLS = 1792
RW_LN_EPS = 64e-5
RW_CHUNK = 64

HG_HEADS = 4
HG_DK = 128
HG_WIDTH = HG_HEADS * HG_DK
HG_COLS = 2048
HG_SUB = 16
F_FLOOR = 1e-30

MLA_HEADS = 8
MLA_Q_RANK = 256
MLA_KV_RANK = 128
MLA_NOPE = 64
MLA_ROPE = 32
MLA_V = 64
MLA_PAD = 128
MLA_SEG = 640
MLA_SCALE = (MLA_NOPE + MLA_ROPE) ** -0.5
MLA_EXP_SCALE = MLA_SCALE * math.log2(math.e)
ROPE_THETA = 10000.0
NEG_BIG = -1e30

D_FF = 2816
GATE_COLS = 3 * D_MODEL

VMEM_LIMIT = 56 * 1024 * 1024


def _cparams(sem):
    return pltpu.CompilerParams(dimension_semantics=sem, vmem_limit_bytes=VMEM_LIMIT)


def _dot(a, b, precision=None):
    return jnp.dot(a, b, precision=precision, preferred_element_type=F32)


def _dot_nt(a, b, precision=None):
    return lax.dot_general(a, b, (((1,), (1,)), ((), ())), precision=precision, preferred_element_type=F32)


def _dot_tn(a, b, precision=None):
    return lax.dot_general(a, b, (((0,), (0,)), ((), ())), precision=precision, preferred_element_type=F32)


def _rms(x):
    return x * lax.rsqrt(jnp.mean(x * x, axis=-1, keepdims=True) + NORM_EPS)


def _adaln_kernel(c_ref, w_ref, b_ref, o_ref):
    c = c_ref[...]
    cond = c * jax.nn.sigmoid(c)
    o_ref[0] = _dot(cond, w_ref[0], HI) + b_ref[0]


def _adaln(c, ada_w, ada_b):
    L, D, N = ada_w.shape
    B = c.shape[0]
    tn = 1536
    return pl.pallas_call(
        _adaln_kernel,
        out_shape=jax.ShapeDtypeStruct((L, B, N), F32),
        grid=(L, N // tn),
        in_specs=[
            pl.BlockSpec((B, D), lambda l, j: (0, 0)),
            pl.BlockSpec((1, D, tn), lambda l, j: (l, 0, j)),
            pl.BlockSpec((1, 1, tn), lambda l, j: (l, 0, j)),
        ],
        out_specs=pl.BlockSpec((1, B, tn), lambda l, j: (l, 0, j)),
        compiler_params=_cparams(("arbitrary", "arbitrary")),
        name="adaln",
    )(c, ada_w, ada_b.reshape(L, 1, N))


def _modmm_kernel(x_ref, sh_ref, sc_ref, w_ref, o_ref, h_ref):
    @pl.when(pl.program_id(1) == 0)
    def _():
        h = _rms(x_ref[...]) * (1.0 + sc_ref[0]) + sh_ref[0]
        h_ref[...] = h.astype(BF16)

    o_ref[...] = _dot(h_ref[...], w_ref[...]).astype(o_ref.dtype)


def _modmm(x, shift, scale, w, S, tm, tn, name):
    T, D = x.shape
    N = w.shape[1]
    tm = min(tm, S)
    return pl.pallas_call(
        _modmm_kernel,
        out_shape=jax.ShapeDtypeStruct((T, N), F32),
        grid=(T // tm, N // tn),
        in_specs=[
            pl.BlockSpec((tm, D), lambda i, j: (i, 0)),
            pl.BlockSpec((1, 1, D), lambda i, j: ((i * tm) // S, 0, 0)),
            pl.BlockSpec((1, 1, D), lambda i, j: ((i * tm) // S, 0, 0)),
            pl.BlockSpec((D, tn), lambda i, j: (0, j)),
        ],
        out_specs=pl.BlockSpec((tm, tn), lambda i, j: (i, j)),
        scratch_shapes=[pltpu.VMEM((tm, D), BF16)],
        compiler_params=_cparams(("arbitrary", "arbitrary")),
        name=name,
    )(x, shift, scale, w)


def _rwkv_prep_kernel(u_ref, mu_ref, w0_ref, w2_ref, a0_ref, a2_ref, g2_ref, kk_ref, ka_ref, bd_ref,
                      r_out, k_out, v_out, lw_out, kn_out, bn_out, g_out, carry_ref, *, blocks_per_seq):
    @pl.when(pl.program_id(0) % blocks_per_seq == 0)
    def _():
        carry_ref[...] = jnp.zeros_like(carry_ref)

    u = u_ref[...]
    tm = u.shape[0]
    row = lax.broadcasted_iota(jnp.int32, u.shape, 0)
    prev = jnp.where(row == 0, carry_ref[...], pltpu.roll(u, 1, 0))
    carry_ref[...] = u[tm - 1:tm, :]
    xm = u + (prev - u) * mu_ref[...]
    W = RW_WIDTH
    r = xm[:, 0:W]
    k = xm[:, W:2 * W]
    v = xm[:, 2 * W:3 * W]
    la = xm[:, 3 * W:3 * W + 128]
    gl = xm[:, 3 * W + 128:3 * W + 256]
    w = -jax.nn.softplus(-(w0_ref[...] + _dot(jnp.tanh(la).astype(BF16), w2_ref[...]))) - 0.5
    lw = -jnp.exp(w)
    a = jax.nn.sigmoid(a0_ref[...] + _dot(la.astype(BF16), a2_ref[...]))
    g = _dot(jax.nn.sigmoid(gl).astype(BF16), g2_ref[...])
    kk = k * kk_ref[...]
    k2hi = (kk * kk).astype(BF16)
    k2lo = (kk * kk - k2hi.astype(F32)).astype(BF16)
    bd = bd_ref[...]
    ss = _dot(k2hi, bd) + _dot(k2lo, bd)
    kn = kk / jnp.maximum(jnp.sqrt(ss), 1e-12)
    k2 = k * (1.0 + (a - 1.0) * ka_ref[...])
    r_out[...] = r
    k_out[...] = k2
    v_out[...] = v
    lw_out[...] = lw
    kn_out[...] = kn
    bn_out[...] = kn * a
    g_out[...] = g


def _rwkv_prep(u_rw, mu, w0, w2p, a0, a2p, g2, k_k, k_a, bd, S, tm):
    T = u_rw.shape[0]
    tm = min(tm, S)
    W = RW_WIDTH
    full = lambda shape: pl.BlockSpec(shape, lambda i: (0,) * len(shape))
    row_spec = pl.BlockSpec((tm, W), lambda i: (i, 0))
    return pl.pallas_call(
        functools.partial(_rwkv_prep_kernel, blocks_per_seq=S // tm),
        out_shape=[jax.ShapeDtypeStruct((T, W), F32)] * 7,
        grid=(T // tm,),
        in_specs=[
            pl.BlockSpec((tm, RW_COLS), lambda i: (i, 0)),
            full((1, RW_COLS)), full((1, W)), full((128, W)), full((1, W)), full((128, W)), full((128, W)),
            full((1, W)), full((1, W)), full((W, W)),
        ],
        out_specs=[row_spec] * 7,
        scratch_shapes=[pltpu.VMEM((1, RW_COLS), F32)],
        compiler_params=_cparams(("arbitrary",)),
        name="rwkv_prep",
    )(u_rw, mu, w0, w2p, a0, a2p, g2, k_k, k_a, bd)


def _rwkv_chunk(r_ref, k_ref, v_ref, lw_ref, kn_ref, bn_ref, g_ref, lnw_ref, lnb_ref, rk_ref, tri_ref, bd_ref,
                o_ref, s_ref, y_ref, fill=lambda: None):
    C = RW_CHUNK
    N = RW_N
    B = r_ref.shape[0]
    tri = tri_ref[...]
    bd = bd_ref[...]
    row = lax.broadcasted_iota(jnp.int32, (C, 2 * C), 0)
    col = lax.broadcasted_iota(jnp.int32, (C, 2 * C), 1) % C
    strict = row > col
    incl = row >= col
    eye = (lax.broadcasted_iota(jnp.int32, (C, C), 0) == lax.broadcasted_iota(jnp.int32, (C, C), 1)).astype(F32)

    wide = []
    for b in range(B):
        r = r_ref[b]
        k = k_ref[b]
        v = v_ref[b]
        lw = lw_ref[b]
        kn = kn_ref[b]
        bn = bn_ref[b]
        lw1 = lw.astype(BF16)
        rem = lw - lw1.astype(F32)
        lw2 = rem.astype(BF16)
        lw3 = (rem - lw2.astype(F32)).astype(BF16)
        c = _dot(tri, lw1) + _dot(tri, lw2) + _dot(tri, lw3)
        c_end = c[C - 1:C, :]
        e_neg = jnp.exp(-c)
        e_end = jnp.exp(c_end - c)
        wide.append(dict(
            a=(-kn * jnp.exp(c - lw)).astype(BF16),
            b=(bn * e_neg).astype(BF16),
            k=(k * e_neg).astype(BF16),
            r=(r * jnp.exp(c)).astype(BF16),
            bh=(bn * e_end).astype(BF16),
            kh=(k * e_end).astype(BF16),
            v=v.astype(BF16),
            g_end=jnp.exp(c_end)))

    chains = [(b, h) for b in range(B) for h in range(RW_HEADS)]
    sl = lambda h: slice(h * N, (h + 1) * N)
    m4 = [_dot_nt(jnp.concatenate([wide[b]['a'][:, sl(h)], wide[b]['r'][:, sl(h)]], axis=0),
                  jnp.concatenate([wide[b]['b'][:, sl(h)], wide[b]['k'][:, sl(h)]], axis=0)) for b, h in chains]
    fill()
    abk = [jnp.where(strict, m[0:C, :], 0.0) for m in m4]
    rbk = [jnp.where(incl, m[C:2 * C, :], 0.0).astype(BF16) for m in m4]
    npow = [m[:, 0:C] for m in abk]
    tinv = [eye + n for n in npow]
    for _ in range(5):
        nb = [n.astype(BF16) for n in npow]
        npow = [_dot(n, n) for n in nb]
        fill()
        tinv = [t + _dot(t.astype(BF16), n.astype(BF16)) for t, n in zip(tinv, npow)]
        fill()
    s0 = [s_ref[b * RW_HEADS + h] for b, h in chains]
    s0b = [s.astype(BF16) for s in s0]
    x = [_dot_nt(wide[b]['a'][:, sl(h)], s) for (b, h), s in zip(chains, s0b)]
    fill()
    x = [xi + _dot(m[:, C:2 * C].astype(BF16), wide[b]['v'][:, sl(h)]) for (b, h), xi, m in zip(chains, x, abk)]
    fill()
    u = [_dot(t.astype(BF16), xi.astype(BF16)) for t, xi in zip(tinv, x)]
    fill()
    uv = [jnp.concatenate([ui.astype(BF16), wide[b]['v'][:, sl(h)]], axis=0) for (b, h), ui in zip(chains, u)]
    y = [_dot_nt(wide[b]['r'][:, sl(h)], s) for (b, h), s in zip(chains, s0b)]
    fill()
    y = [yi + _dot(m, w) for yi, m, w in zip(y, rbk, uv)]
    fill()
    snew = [_dot_tn(w, jnp.concatenate([wide[b]['bh'][:, sl(h)], wide[b]['kh'][:, sl(h)]], axis=0))
            for (b, h), w in zip(chains, uv)]
    fill()
    for (b, h), s, sn, yi in zip(chains, s0, snew, y):
        s_ref[b * RW_HEADS + h] = s * wide[b]['g_end'][:, sl(h)] + sn
        y_ref[b, :, sl(h)] = yi

    rows = lambda ref: jnp.concatenate([ref[b] for b in range(B)], axis=0)
    r, k, v, y = rows(r_ref), rows(k_ref), rows(v_ref), rows(y_ref)
    mean = _dot(y.astype(BF16), bd) * (1.0 / N)
    d = y - mean
    var = _dot((d * d).astype(BF16), bd) * (1.0 / N)
    yn = d * lax.rsqrt(var + RW_LN_EPS) * lnw_ref[...] + lnb_ref[...]
    bonus = _dot((r * k * rk_ref[...]).astype(BF16), bd)
    out = (yn + bonus * v) * rows(g_ref)
    for b in range(B):
        o_ref[b] = out[b * C:(b + 1) * C, :]


def _rwkv_scan_kernel(*refs):
    s_ref = refs[13]

    @pl.when(pl.program_id(0) == 0)
    def _():
        s_ref[...] = jnp.zeros_like(s_ref)

    _rwkv_chunk(*refs)


def _rwkv_scan(r, k, v, lw, kn, bn, g, ln_w, ln_b, r_k, tri, bd, B, S):
    T, W = r.shape
    C = RW_CHUNK
    blk = pl.BlockSpec((B, C, W), lambda c: (0, c, 0))
    full = lambda shape: pl.BlockSpec(shape, lambda c: (0,) * len(shape))
    seq = lambda t: t.reshape(B, S, W)
    out = pl.pallas_call(
        _rwkv_scan_kernel,
        out_shape=jax.ShapeDtypeStruct((B, S, W), F32),
        grid=(S // C,),
        in_specs=[blk] * 7 + [full((1, W))] * 3 + [full((C, C)), full((W, W))],
        out_specs=blk,
        scratch_shapes=[pltpu.VMEM((B * RW_HEADS, RW_N, RW_N), F32), pltpu.VMEM((B, C, W), F32)],
        compiler_params=_cparams(("arbitrary",)),
        name="rwkv_scan",
    )(seq(r), seq(k), seq(v), seq(lw), seq(kn), seq(bn), seq(g), ln_w, ln_b, r_k, tri, bd)
    return out.reshape(T, W)


def _hgrn_kernel(q_ref, fz_ref, i_ref, og_ref, lb_ref, nw_ref, tri_ref, o_ref, st_ref, *, layer):
    @pl.when(pl.program_id(1) == 0)
    def _():
        st_ref[...] = jnp.zeros_like(st_ref)

    lbraw = lb_ref[...]
    e = jnp.exp(lbraw - jnp.max(lbraw, axis=0, keepdims=True))
    p = e / jnp.sum(e, axis=0, keepdims=True)
    lb = jnp.zeros((1, p.shape[1]), F32)
    for l in range(1, layer + 1):
        lb = lb + p[l:l + 1, :]
    fz = fz_ref[...]
    f = lb + (1.0 - lb) * jax.nn.sigmoid(fz)
    g = jnp.log(jnp.maximum(f, F_FLOOR))
    key = (1.0 - lb) * jax.nn.sigmoid(-fz)
    tri = tri_ref[...]
    g1 = g.astype(BF16)
    rem = g - g1.astype(F32)
    g2 = rem.astype(BF16)
    g3 = (rem - g2.astype(F32)).astype(BF16)
    bcum = _dot(tri, g1) + _dot(tri, g2) + _dot(tri, g3)
    bk = bcum - jnp.log(key)
    q = q_ref[...]
    v = i_ref[...]
    og = og_ref[...]
    qe = q * jnp.exp(bcum)
    rt = q.shape[0]
    n = HG_SUB
    rowi = lax.broadcasted_iota(jnp.int32, (n, HG_DK), 0)
    chunks = [slice(c * n, (c + 1) * n) for c in range(rt // n)]
    outs = []
    for h in range(HG_HEADS):
        hs = slice(h * HG_DK, (h + 1) * HG_DK)
        b_last = [bcum[rs, hs][n - 1:n, :] for rs in chunks]
        inc = [_dot_tn(v[rs, hs].astype(BF16), (key[rs, hs] * jnp.exp(bl - bcum[rs, hs])).astype(BF16))
               for rs, bl in zip(chunks, b_last)]
        intra = []
        for rs in chunks:
            bc, bks, qc, vc = bcum[rs, hs], bk[rs, hs], q[rs, hs], v[rs, hs]
            o = jnp.zeros((n, HG_DK), F32)
            for s in range(n):
                dec = jnp.exp(jnp.where(rowi >= s, bc - bks[s:s + 1, :], NEG_BIG))
                o = o + jnp.sum(qc * dec, axis=-1, keepdims=True) * vc[s:s + 1, :]
            intra.append(o)
        st = st_ref[h]
        starts = []
        for bl, d in zip(b_last, inc):
            starts.append(st)
            st = st * jnp.exp(bl) + d
        st_ref[h] = st
        inter = [_dot_nt(qe[rs, hs].astype(BF16), s0.astype(BF16)) for rs, s0 in zip(chunks, starts)]
        o = jnp.concatenate([a + b for a, b in zip(inter, intra)], axis=0)
        outs.append(_rms(o) * nw_ref[...] * (og[:, hs] * jax.nn.sigmoid(og[:, hs])))
    o_ref[...] = jnp.concatenate(outs, axis=1)


def _hgrn(u_hg, lb_raw, norm_w, tri, layer, B, S, rt):
    T = u_hg.shape[0]
    rt = min(rt, S)
    nR = S // rt
    L = lb_raw.shape[0]
    W = HG_WIDTH
    col = lambda j: pl.BlockSpec((rt, W), lambda b, i: (b * nR + i, j))
    return pl.pallas_call(
        functools.partial(_hgrn_kernel, layer=layer),
        out_shape=jax.ShapeDtypeStruct((T, W), F32),
        grid=(B, nR),
        in_specs=[
            col(0), col(1), col(2), col(3),
            pl.BlockSpec((L, W), lambda b, i: (0, 0)),
            pl.BlockSpec((1, HG_DK), lambda b, i: (0, 0)),
            pl.BlockSpec((rt, rt), lambda b, i: (0, 0)),
        ],
        out_specs=pl.BlockSpec((rt, W), lambda b, i: (b * nR + i, 0)),
        scratch_shapes=[pltpu.VMEM((HG_HEADS, HG_DK, HG_DK), F32)],
        compiler_params=_cparams(("arbitrary", "arbitrary")),
        name="hgrn",
    )(u_hg, u_hg, u_hg, u_hg, lb_raw, norm_w, tri)


def _mla_prep_kernel(u_ref, pos_ref, invf_ref, qnw_ref, kvnw_ref, wq_ref, wqs_ref, wk_ref, wvt_ref,
                     q_out, k_out, vt_out):
    u = u_ref[...]
    cq = u[:, 0:MLA_Q_RANK]
    ckv = u[:, MLA_Q_RANK:MLA_Q_RANK + MLA_KV_RANK]
    kr = u[:, 384:512]
    krs = u[:, 512:640]
    ang = pos_ref[...].astype(F32) * invf_ref[...]
    lane = lax.broadcasted_iota(jnp.int32, ang.shape, 1)
    cosv = jnp.cos(ang)
    sinv = jnp.sin(ang)
    cos_t = jnp.where(lane < MLA_NOPE, 1.0, jnp.where(lane < MLA_NOPE + MLA_ROPE, cosv, 0.0))
    sin_t = jnp.where(lane < MLA_NOPE, 0.0,
                      jnp.where(lane < MLA_NOPE + MLA_ROPE // 2, -sinv,
                                jnp.where(lane < MLA_NOPE + MLA_ROPE, sinv, 0.0)))
    cqn = (_rms(cq) * qnw_ref[...]).astype(BF16)
    kvn = (_rms(ckv) * kvnw_ref[...]).astype(BF16)
    qm = _dot(cqn, wq_ref[...])
    qs = _dot(cqn, wqs_ref[...])
    kn = _dot(kvn, wk_ref[...])
    vt = _dot_nt(wvt_ref[...], kvn)
    k_rot = kr * cos_t + krs * sin_t
    for h in range(MLA_HEADS):
        sl = slice(h * MLA_PAD, (h + 1) * MLA_PAD)
        q_out[:, sl] = ((qm[:, sl] * cos_t + qs[:, sl] * sin_t) * MLA_EXP_SCALE).astype(BF16)
        k_out[:, sl] = (kn[:, sl] + k_rot).astype(BF16)
    vrow = lax.broadcasted_iota(jnp.int32, vt.shape, 0) % MLA_PAD
    vt_out[...] = jnp.where(vrow == MLA_V, 1.0, vt).astype(BF16)


def _mla_prep(u_mla, pos, invf, qnw, kvnw, wq, wqs, wk, wvt, tm):
    T = u_mla.shape[0]
    tm = min(tm, T)
    W = MLA_HEADS * MLA_PAD
    full = lambda shape: pl.BlockSpec(shape, lambda i: (0,) * len(shape))
    out_spec = pl.BlockSpec((tm, W), lambda i: (i, 0))
    return pl.pallas_call(
        _mla_prep_kernel,
        out_shape=[jax.ShapeDtypeStruct((T, W), BF16)] * 2 + [jax.ShapeDtypeStruct((W, T), BF16)],
        grid=(T // tm,),
        in_specs=[
            pl.BlockSpec((tm, MLA_SEG), lambda i: (i, 0)),
            pl.BlockSpec((tm, 1), lambda i: (i, 0)),
            full((1, MLA_PAD)), full((1, MLA_Q_RANK)), full((1, MLA_KV_RANK)),
            full((MLA_Q_RANK, W)), full((MLA_Q_RANK, W)), full((MLA_KV_RANK, W)), full((W, MLA_KV_RANK)),
        ],
        out_specs=[out_spec] * 2 + [pl.BlockSpec((W, tm), lambda i: (0, i))],
        compiler_params=_cparams(("arbitrary",)),
        name="mla_prep",
    )(u_mla, pos, invf, qnw, kvnw, wq, wqs, wk, wvt)


def _attn_kernel(qi_ref, ki_ref, q_ref, k_ref, vt_ref, o_ref, m_ref, acc_ref, *, groups):
    t = pl.program_id(2)
    qi = qi_ref[t]
    ki = ki_ref[t]
    blk = q_ref.shape[0]
    gw = blk // groups
    P = MLA_PAD

    @pl.when(ki == 0)
    def _():
        m_ref[...] = jnp.full_like(m_ref, NEG_BIG)
        acc_ref[...] = jnp.zeros_like(acc_ref)

    def step(diag):
        kv_len = lambda g: (g + 1) * gw if diag else blk
        hs = lambda h: slice(h * P, (h + 1) * P)

        def scores(h, g):
            return _dot_nt(k_ref[0:kv_len(g), hs(h)], q_ref[g * gw:(g + 1) * gw, hs(h)])

        def softmax_pv(h, g, s):
            if diag:
                row = lax.broadcasted_iota(jnp.int32, s.shape, 0)
                col = lax.broadcasted_iota(jnp.int32, s.shape, 1) + g * gw
                s = jnp.where(row <= col, s, NEG_BIG)
            cs = slice(g * gw, (g + 1) * gw)
            m_prev = m_ref[h, :, cs]
            m_new = jnp.maximum(m_prev, jnp.max(s, axis=0, keepdims=True))
            p = jnp.exp2(s - m_new).astype(BF16)
            acc_ref[h, :, cs] = (jnp.exp2(m_prev - m_new) * acc_ref[h, :, cs]
                                 + _dot(vt_ref[hs(h), 0:kv_len(g)], p))
            m_ref[h, :, cs] = m_new

        s0 = [scores(0, g) for g in range(groups)]
        s1 = []
        for g in range(groups):
            softmax_pv(0, g, s0[g])
            s1.append(scores(1, g))
        for g in range(groups):
            softmax_pv(1, g, s1[g])

    @pl.when(ki < qi)
    def _():
        step(False)

    @pl.when(ki == qi)
    def _():
        step(True)
        for h in range(2):
            acc = acc_ref[h]
            o_ref[:, h * P:(h + 1) * P] = (acc / acc[MLA_V:MLA_V + 1, :]).T.astype(o_ref.dtype)


def _mla_attn(q, k, vt, B, S, blk):
    T, W = q.shape
    blk = min(blk, S)
    nq = S // blk
    groups = max(1, min(8, blk // 128))
    qi = np.array([i for i in range(nq) for _ in range(i + 1)], np.int32)
    ki = np.array([j for i in range(nq) for j in range(i + 1)], np.int32)
    P2 = 2 * MLA_PAD
    grid_spec = pltpu.PrefetchScalarGridSpec(
        num_scalar_prefetch=2,
        grid=(B, MLA_HEADS // 2, len(qi)),
        in_specs=[
            pl.BlockSpec((blk, P2), lambda b, h, t, qi, ki: (b * nq + qi[t], h)),
            pl.BlockSpec((blk, P2), lambda b, h, t, qi, ki: (b * nq + ki[t], h)),
            pl.BlockSpec((P2, blk), lambda b, h, t, qi, ki: (h, b * nq + ki[t])),
        ],
        out_specs=pl.BlockSpec((blk, P2), lambda b, h, t, qi, ki: (b * nq + qi[t], h)),
        scratch_shapes=[pltpu.VMEM((2, 1, blk), F32), pltpu.VMEM((2, MLA_PAD, blk), F32)],
    )
    return pl.pallas_call(
        functools.partial(_attn_kernel, groups=groups),
        out_shape=jax.ShapeDtypeStruct((T, W), BF16),
        grid_spec=grid_spec,
        compiler_params=_cparams(("arbitrary", "arbitrary", "arbitrary")),
        name="mla_attn",
    )(jnp.asarray(qi), jnp.asarray(ki), q, k, vt)


def _merge_kernel(x_ref, gt_ref, oa_ref, ob_ref, oc_ref, pa_ref, pb_ref, pc_ref, wo_ref, g1_ref, o_ref):
    D = D_MODEL
    gt = gt_ref[...]
    y = jax.nn.sigmoid(gt[:, 0:D]) * _dot(oa_ref[...].astype(BF16), pa_ref[...])
    y = y + jax.nn.sigmoid(gt[:, D:2 * D]) * _dot(ob_ref[...].astype(BF16), pb_ref[...])
    y = y + jax.nn.sigmoid(gt[:, 2 * D:3 * D]) * _dot(oc_ref[...], pc_ref[...])
    o_ref[...] = x_ref[...] + g1_ref[0] * _dot(y.astype(BF16), wo_ref[...])


def _merge(x, u_gate, o_a, o_b, o_c, pa, pb, pc, wo, g1, S, tm):
    T, D = x.shape
    tm = min(tm, S)
    rows = lambda w: pl.BlockSpec((tm, w), lambda i: (i, 0))
    full = lambda shape: pl.BlockSpec(shape, lambda i: (0,) * len(shape))
    return pl.pallas_call(
        _merge_kernel,
        out_shape=jax.ShapeDtypeStruct((T, D), F32),
        grid=(T // tm,),
        in_specs=[
            rows(D), rows(GATE_COLS), rows(RW_WIDTH), rows(HG_WIDTH), rows(MLA_HEADS * MLA_PAD),
            full(pa.shape), full(pb.shape), full(pc.shape), full(wo.shape),
            pl.BlockSpec((1, 1, D), lambda i: ((i * tm) // S, 0, 0)),
        ],
        out_specs=rows(D),
        compiler_params=_cparams(("arbitrary",)),
        name="merge",
    )(x, u_gate, o_a, o_b, o_c, pa, pb, pc, wo, g1)


def _ffn_up_kernel(x_ref, sh_ref, sc_ref, wv_ref, wg_ref, cwv_ref, cwg_ref, cbv_ref, cbg_ref, o_ref,
                   h_ref, carry_ref, *, blocks_per_seq):
    j = pl.program_id(1)

    @pl.when(j == 0)
    def _():
        h = _rms(x_ref[...]) * (1.0 + sc_ref[0]) + sh_ref[0]
        h_ref[...] = h.astype(BF16)

    @pl.when(pl.program_id(0) % blocks_per_seq == 0)
    def _():
        carry_ref[j] = jnp.zeros(carry_ref.shape[1:], F32)

    h = h_ref[...]
    tm = h.shape[0]
    carry = carry_ref[j]

    def conv(u, cw_ref, cb_ref, c2, c1):
        row = lax.broadcasted_iota(jnp.int32, (8, u.shape[1]), 0)
        r1 = pltpu.roll(u, 1, 0)
        r2 = pltpu.roll(u, 2, 0)
        u1 = jnp.concatenate([jnp.where(row == 0, c1, r1[0:8, :]), r1[8:, :]], axis=0)
        u2 = jnp.concatenate([jnp.where(row == 0, c2, jnp.where(row == 1, c1, r2[0:8, :])), r2[8:, :]], axis=0)
        cw = cw_ref[...]
        return cw[0:1, :] * u2 + cw[1:2, :] * u1 + cw[2:3, :] * u + cb_ref[...]

    uv = _dot(h, wv_ref[...])
    ug = _dot(h, wg_ref[...])
    val = conv(uv, cwv_ref, cbv_ref, carry[0:1, :], carry[1:2, :])
    gate = conv(ug, cwg_ref, cbg_ref, carry[2:3, :], carry[3:4, :])
    carry_ref[j] = jnp.concatenate([uv[tm - 2:tm, :], ug[tm - 2:tm, :]], axis=0)
    o_ref[...] = (gate * jax.nn.sigmoid(gate) * val).astype(o_ref.dtype)


def _ffn_up(x, shift, scale, wv, wg, cwv, cwg, cbv, cbg, S, tm, tn):
    T, D = x.shape
    tm = min(tm, S)
    ncol = D_FF // tn
    return pl.pallas_call(
        functools.partial(_ffn_up_kernel, blocks_per_seq=S // tm),
        out_shape=jax.ShapeDtypeStruct((T, D_FF), BF16),
        grid=(T // tm, ncol),
        in_specs=[
            pl.BlockSpec((tm, D), lambda i, j: (i, 0)),
            pl.BlockSpec((1, 1, D), lambda i, j: ((i * tm) // S, 0, 0)),
            pl.BlockSpec((1, 1, D), lambda i, j: ((i * tm) // S, 0, 0)),
            pl.BlockSpec((D, tn), lambda i, j: (0, j)),
            pl.BlockSpec((D, tn), lambda i, j: (0, j)),
            pl.BlockSpec((3, tn), lambda i, j: (0, j)),
            pl.BlockSpec((3, tn), lambda i, j: (0, j)),
            pl.BlockSpec((1, tn), lambda i, j: (0, j)),
            pl.BlockSpec((1, tn), lambda i, j: (0, j)),
        ],
        out_specs=pl.BlockSpec((tm, tn), lambda i, j: (i, j)),
        scratch_shapes=[pltpu.VMEM((tm, D), BF16), pltpu.VMEM((ncol, 4, tn), F32)],
        compiler_params=_cparams(("arbitrary", "arbitrary")),
        name="ffn_up",
    )(x, shift, scale, wv, wg, cwv, cwg, cbv, cbg)


def _ffn_down_kernel(x_ref, a_ref, w_ref, g2_ref, fw_ref, o_ref, *, final):
    y = x_ref[...] + g2_ref[0] * _dot(a_ref[...], w_ref[...])
    if final:
        y = _rms(y) * fw_ref[...]
    o_ref[...] = y


def _ffn_down(x, act, w_down, g2, final_w, S, tm, final):
    T, D = x.shape
    tm = min(tm, S)
    return pl.pallas_call(
        functools.partial(_ffn_down_kernel, final=final),
        out_shape=jax.ShapeDtypeStruct((T, D), F32),
        grid=(T // tm,),
        in_specs=[
            pl.BlockSpec((tm, D), lambda i: (i, 0)),
            pl.BlockSpec((tm, D_FF), lambda i: (i, 0)),
            pl.BlockSpec((D_FF, D), lambda i: (0, 0)),
            pl.BlockSpec((1, 1, D), lambda i: ((i * tm) // S, 0, 0)),
            pl.BlockSpec((1, D), lambda i: (0, 0)),
        ],
        out_specs=pl.BlockSpec((tm, D), lambda i: (i, 0)),
        compiler_params=_cparams(("arbitrary",)),
        name="ffn_down",
    )(x, act, w_down, g2, final_w)


def _tiles(S):
    return dict(inproj=min(1024, S), rwkv_prep=min(512, S), hgrn=min(256, S), mla_prep=min(512, S),
                attn=min(2048, S), merge=min(512, S), ffn=min(512, S))


def _head_pad(w, per_head, lo, hi, dst, pad=MLA_PAD):
    K = w.shape[0]
    w = w.reshape(K, MLA_HEADS, per_head)[:, :, lo:hi]
    out = jnp.zeros((K, MLA_HEADS, pad), w.dtype)
    out = out.at[:, :, dst:dst + (hi - lo)].set(w)
    return out.reshape(K, MLA_HEADS * pad)


def _swap_halves(w):
    half = w.shape[-1] // 2
    return jnp.concatenate([w[..., half:], w[..., :half]], axis=-1)


def kernel(x, c, positions, ada_w, ada_b, w_in, rwkv_mu, rwkv_w0, rwkv_w2, rwkv_a0, rwkv_a2, rwkv_g2, rwkv_k_k, rwkv_k_a, rwkv_r_k, rwkv_ln_w, rwkv_ln_b, hgrn_lb, hgrn_norm_w, mla_q_norm_w, mla_w_uq, mla_kv_norm_w, mla_w_ukv, branch_proj_a, branch_proj_b, branch_proj_c, w_out, ffn_w_up, ffn_conv_w, ffn_conv_b, ffn_w_down, final_norm_w):
    B, S, D = x.shape
    L = ada_w.shape[0]
    T = B * S
    xt = x.reshape(T, D)
    pos = positions.reshape(T, 1)

    mod = _adaln(c, ada_w, ada_b).reshape(L, B, 6, 1, D)

    inv_freq = ROPE_THETA ** (-jnp.arange(0, MLA_ROPE, 2, dtype=F32) / MLA_ROPE)
    invf = jnp.zeros((1, MLA_PAD), F32).at[0, MLA_NOPE:MLA_NOPE + MLA_ROPE].set(jnp.concatenate([inv_freq, inv_freq]))

    ii = np.arange(RW_WIDTH)
    bd = jnp.asarray((ii[:, None] // RW_N == ii[None, :] // RW_N).astype(np.float32))
    cc = np.arange(RW_CHUNK)
    tri_rw = jnp.asarray((cc[:, None] >= cc[None, :]).astype(np.float32)).astype(BF16)
    bd_b = bd.astype(BF16)
    tl = _tiles(S)
    rt_hg = tl['hgrn']
    rr = np.arange(rt_hg)
    tri_hg = jnp.asarray(((rr[:, None] >= rr[None, :])
                          & (rr[:, None] // HG_SUB == rr[None, :] // HG_SUB)).astype(np.float32)).astype(BF16)

    o0 = RW_COLS
    o1 = o0 + HG_COLS
    o2 = o1 + MLA_Q_RANK + MLA_KV_RANK + MLA_ROPE
    for l in range(L):
        sh1, sc1, g1, sh2, sc2, g2 = (mod[l, :, j] for j in range(6))
        wl = w_in[l]
        w_rw = wl[:, 0:o0].astype(BF16)
        w_hg = wl[:, o0:o1].astype(BF16)
        w_gate = wl[:, o2:o2 + GATE_COLS].astype(BF16)
        w_kr = wl[:, o2 - MLA_ROPE:o2]
        lanes = jnp.zeros((D, MLA_PAD), F32)
        w_mla = jnp.concatenate([
            wl[:, o1:o1 + MLA_Q_RANK + MLA_KV_RANK],
            lanes.at[:, MLA_NOPE:MLA_NOPE + MLA_ROPE].set(w_kr),
            lanes.at[:, MLA_NOPE:MLA_NOPE + MLA_ROPE].set(_swap_halves(w_kr)),
        ], axis=1).astype(BF16)

        u_rw = _modmm(xt, sh1, sc1, w_rw, S, tl['inproj'], RW_COLS, "inproj_rw")
        u_hg = _modmm(xt, sh1, sc1, w_hg, S, tl['inproj'], HG_COLS, "inproj_hg")
        u_gate = _modmm(xt, sh1, sc1, w_gate, S, tl['inproj'], GATE_COLS // 2, "inproj_gate")
        u_mla = _modmm(xt, sh1, sc1, w_mla, S, tl['inproj'], MLA_SEG, "inproj_mla")

        zpad = jnp.zeros((64, RW_WIDTH), F32)
        r, k2, v, lw, kn, bn, g = _rwkv_prep(
            u_rw, rwkv_mu[l].reshape(1, -1), rwkv_w0[l].reshape(1, -1),
            jnp.concatenate([rwkv_w2[l], zpad], axis=0).astype(BF16), rwkv_a0[l].reshape(1, -1),
            jnp.concatenate([zpad, rwkv_a2[l]], axis=0).astype(BF16), rwkv_g2[l].astype(BF16),
            rwkv_k_k[l].reshape(1, -1), rwkv_k_a[l].reshape(1, -1), bd_b, S, tl['rwkv_prep'])
        o_a = _rwkv_scan(r, k2, v, lw, kn, bn, g, rwkv_ln_w[l].reshape(1, -1), rwkv_ln_b[l].reshape(1, -1),
                         rwkv_r_k[l].reshape(1, -1), tri_rw, bd_b, B, S)

        o_b = _hgrn(u_hg, hgrn_lb, hgrn_norm_w[l].reshape(1, -1), tri_hg, l, B, S, rt_hg)

        per_q = MLA_NOPE + MLA_ROPE
        wq = _head_pad(mla_w_uq[l], per_q, 0, per_q, 0).astype(BF16)
        wq_rope = mla_w_uq[l].reshape(MLA_Q_RANK, MLA_HEADS, per_q)[:, :, MLA_NOPE:]
        wqs = _head_pad(_swap_halves(wq_rope).reshape(MLA_Q_RANK, -1), MLA_ROPE, 0, MLA_ROPE, MLA_NOPE).astype(BF16)
        per_kv = MLA_NOPE + MLA_V
        wk = _head_pad(mla_w_ukv[l], per_kv, 0, MLA_NOPE, 0).astype(BF16)
        wvt = _head_pad(mla_w_ukv[l], per_kv, MLA_NOPE, per_kv, 0).T.astype(BF16)
        q_cat, k_cat, v_t = _mla_prep(u_mla, pos, invf, mla_q_norm_w[l].reshape(1, -1),
                                      mla_kv_norm_w[l].reshape(1, -1), wq, wqs, wk, wvt, tl['mla_prep'])
        o_c = _mla_attn(q_cat, k_cat, v_t, B, S, tl['attn'])

        pc = jnp.zeros((MLA_HEADS, MLA_PAD, D), F32).at[:, :MLA_V, :].set(
            branch_proj_c[l].reshape(MLA_HEADS, MLA_V, D)).reshape(MLA_HEADS * MLA_PAD, D).astype(BF16)
        xt = _merge(xt, u_gate, o_a, o_b, o_c, branch_proj_a[l].astype(BF16), branch_proj_b[l].astype(BF16), pc,
                    w_out[l].astype(BF16), g1, S, tl['merge'])

        wu = ffn_w_up[l]
        cw = ffn_conv_w[l]
        cb = ffn_conv_b[l].reshape(1, -1)
        act = _ffn_up(xt, sh2, sc2, wu[:, :D_FF].astype(BF16), wu[:, D_FF:].astype(BF16),
                      cw[:, :D_FF], cw[:, D_FF:], cb[:, :D_FF], cb[:, D_FF:], S, tl['ffn'], D_FF // 2)
        xt = _ffn_down(xt, act, ffn_w_down[l].astype(BF16), g2, final_norm_w.reshape(1, -1), S, tl['ffn'], l == L - 1)

    return xt.reshape(B, S, D)
```

```python
import functools
import math

import numpy as np
import jax
import jax.numpy as jnp
from jax import lax
from jax.experimental import pallas as pl
from jax.experimental.pallas import tpu as pltpu

F32 = jnp.float32
BF16 = jnp.bfloat16
HI = lax.Precision.HIGHEST

D_MODEL = 1024
NORM_EPS = 1e-6

RW_HEADS = 8
RW_N = 64
RW_WIDTH = RW_HEADS * RW_N
RW_COLS = 1792
RW_LN_EPS = 64e-5
RW_CHUNK = 64

HG_HEADS = 4
HG_DK = 128
HG_WIDTH = HG_HEADS * HG_DK
HG_COLS = 2048
HG_SUB = 16
F_FLOOR = 1e-30

MLA_HEADS = 8
MLA_Q_RANK = 256
MLA_KV_RANK = 128
MLA_NOPE = 64
MLA_ROPE = 32
MLA_V = 64
MLA_PAD = 128
MLA_SEG = 640
MLA_SCALE = (MLA_NOPE + MLA_ROPE) ** -0.5
MLA_EXP_SCALE = MLA_SCALE * math.log2(math.e)
ROPE_THETA = 10000.0
NEG_BIG = -1e30

D_FF = 2816
GATE_COLS = 3 * D_MODEL

U_HG = 0
U_RW = U_HG + HG_COLS
U_MLA = U_RW + RW_COLS
U_COLS = 4608

VMEM_LIMIT = 56 * 1024 * 1024


def _cparams(sem):
    return pltpu.CompilerParams(dimension_semantics=sem, vmem_limit_bytes=VMEM_LIMIT)


def _dot(a, b, precision=None):
    return jnp.dot(a, b, precision=precision, preferred_element_type=F32)


def _dot_nt(a, b, precision=None):
    return lax.dot_general(a, b, (((1,), (1,)), ((), ())), precision=precision, preferred_element_type=F32)


def _dot_tn(a, b, precision=None):
    return lax.dot_general(a, b, (((0,), (0,)), ((), ())), precision=precision, preferred_element_type=F32)


def _rms(x):
    return x * lax.rsqrt(jnp.mean(x * x, axis=-1, keepdims=True) + NORM_EPS)


def _adaln_kernel(c_ref, w_ref, b_ref, o_ref):
    c = c_ref[...]
    cond = c * jax.nn.sigmoid(c)
    o_ref[0] = _dot(cond, w_ref[0], HI) + b_ref[0]


def _adaln(c, ada_w, ada_b):
    L, D, N = ada_w.shape
    B = c.shape[0]
    tn = 1536
    return pl.pallas_call(
        _adaln_kernel,
        out_shape=jax.ShapeDtypeStruct((L, B, N), F32),
        grid=(L, N // tn),
        in_specs=[
            pl.BlockSpec((B, D), lambda l, j: (0, 0)),
            pl.BlockSpec((1, D, tn), lambda l, j: (l, 0, j)),
            pl.BlockSpec((1, 1, tn), lambda l, j: (l, 0, j)),
        ],
        out_specs=pl.BlockSpec((1, B, tn), lambda l, j: (l, 0, j)),
        compiler_params=_cparams(("arbitrary", "arbitrary")),
        name="adaln",
    )(c, ada_w, ada_b.reshape(L, 1, N))


def _modmm_kernel(x_ref, sh_ref, sc_ref, w_ref, o_ref, h_ref, *, gate):
    @pl.when(pl.program_id(1) == 0)
    def _():
        h = _rms(x_ref[...]) * (1.0 + sc_ref[0]) + sh_ref[0]
        h_ref[...] = h.astype(BF16)

    u = _dot(h_ref[...], w_ref[...])
    o_ref[...] = (jax.nn.sigmoid(u) if gate else u).astype(o_ref.dtype)


def _modmm(x, shift, scale, w, S, tm, tn, name, gate=False):
    T, D = x.shape
    N = w.shape[1]
    tm = min(tm, S)
    return pl.pallas_call(
        functools.partial(_modmm_kernel, gate=gate),
        out_shape=jax.ShapeDtypeStruct((T, N), BF16 if gate else F32),
        grid=(T // tm, N // tn),
        in_specs=[
            pl.BlockSpec((tm, D), lambda i, j: (i, 0)),
            pl.BlockSpec((1, 1, D), lambda i, j: ((i * tm) // S, 0, 0)),
            pl.BlockSpec((1, 1, D), lambda i, j: ((i * tm) // S, 0, 0)),
            pl.BlockSpec((D, tn), lambda i, j: (0, j)),
        ],
        out_specs=pl.BlockSpec((tm, tn), lambda i, j: (i, j)),
        scratch_shapes=[pltpu.VMEM((tm, D), BF16)],
        compiler_params=_cparams(("arbitrary", "arbitrary")),
        name=name,
    )(x, shift, scale, w)


def _rwkv_prep_kernel(ur_ref, uk_ref, uv_ref, ul_ref, mu_ref, w0_ref, w2_ref, a0_ref, a2_ref, g2_ref, kk_ref, ka_ref,
                      bd_ref, r_out, k_out, v_out, lw_out, kn_out, bn_out, g_out, carry_ref, *, blocks_per_seq):
    @pl.when(pl.program_id(0) % blocks_per_seq == 0)
    def _():
        carry_ref[...] = jnp.zeros_like(carry_ref)

    u = jnp.concatenate([ur_ref[...], uk_ref[...], uv_ref[...], ul_ref[...]], axis=1)
    tm = u.shape[0]
    row = lax.broadcasted_iota(jnp.int32, u.shape, 0)
    prev = jnp.where(row == 0, carry_ref[...], pltpu.roll(u, 1, 0))
    carry_ref[...] = u[tm - 1:tm, :]
    xm = u + (prev - u) * mu_ref[...]
    W = RW_WIDTH
    r = xm[:, 0:W]
    k = xm[:, W:2 * W]
    v = xm[:, 2 * W:3 * W]
    la = xm[:, 3 * W:3 * W + 128]
    gl = xm[:, 3 * W + 128:3 * W + 256]
    w = -jax.nn.softplus(-(w0_ref[...] + _dot(jnp.tanh(la).astype(BF16), w2_ref[...]))) - 0.5
    lw = -jnp.exp(w)
    a = jax.nn.sigmoid(a0_ref[...] + _dot(la.astype(BF16), a2_ref[...]))
    g = _dot(jax.nn.sigmoid(gl).astype(BF16), g2_ref[...])
    kk = k * kk_ref[...]
    k2hi = (kk * kk).astype(BF16)
    k2lo = (kk * kk - k2hi.astype(F32)).astype(BF16)
    bd = bd_ref[...]
    ss = _dot(k2hi, bd) + _dot(k2lo, bd)
    kn = kk / jnp.maximum(jnp.sqrt(ss), 1e-12)
    k2 = k * (1.0 + (a - 1.0) * ka_ref[...])
    r_out[...] = r
    k_out[...] = k2
    v_out[...] = v
    lw_out[...] = lw
    kn_out[...] = kn
    bn_out[...] = kn * a
    g_out[...] = g


def _rwkv_prep(u, mu, w0, w2p, a0, a2p, g2, k_k, k_a, bd, S, tm):
    T = u.shape[0]
    tm = min(tm, S)
    W = RW_WIDTH
    full = lambda shape: pl.BlockSpec(shape, lambda i: (0,) * len(shape))
    row_spec = pl.BlockSpec((tm, W), lambda i: (i, 0))
    ucol = lambda width, off: pl.BlockSpec((tm, width), lambda i: (i, off // width))
    return pl.pallas_call(
        functools.partial(_rwkv_prep_kernel, blocks_per_seq=S // tm),
        out_shape=[jax.ShapeDtypeStruct((T, W), F32)] * 7,
        grid=(T // tm,),
        in_specs=[
            ucol(W, U_RW), ucol(W, U_RW + W), ucol(W, U_RW + 2 * W), ucol(256, U_RW + 3 * W),
            full((1, RW_COLS)), full((1, W)), full((128, W)), full((1, W)), full((128, W)), full((128, W)),
            full((1, W)), full((1, W)), full((W, W)),
        ],
        out_specs=[row_spec] * 7,
        scratch_shapes=[pltpu.VMEM((1, RW_COLS), F32)],
        compiler_params=_cparams(("arbitrary",)),
        name="rwkv_prep",
    )(u, u, u, u, mu, w0, w2p, a0, a2p, g2, k_k, k_a, bd)


def _rwkv_chunk(r_ref, k_ref, v_ref, lw_ref, kn_ref, bn_ref, g_ref, lnw_ref, lnb_ref, rk_ref, tri_ref, bd_ref,
                o_ref, s_ref, y_ref, fill=lambda: None):
    C = RW_CHUNK
    N = RW_N
    B = r_ref.shape[0]
    tri = tri_ref[...]
    bd = bd_ref[...]
    row = lax.broadcasted_iota(jnp.int32, (C, 2 * C), 0)
    col = lax.broadcasted_iota(jnp.int32, (C, 2 * C), 1) % C
    strict = row > col
    incl = row >= col
    eye = (lax.broadcasted_iota(jnp.int32, (C, C), 0) == lax.broadcasted_iota(jnp.int32, (C, C), 1)).astype(F32)

    wide = []
    for b in range(B):
        r = r_ref[b]
        k = k_ref[b]
        v = v_ref[b]
        lw = lw_ref[b]
        kn = kn_ref[b]
        bn = bn_ref[b]
        lw1 = lw.astype(BF16)
        rem = lw - lw1.astype(F32)
        lw2 = rem.astype(BF16)
        lw3 = (rem - lw2.astype(F32)).astype(BF16)
        c = _dot(tri, lw1) + _dot(tri, lw2) + _dot(tri, lw3)
        c_end = c[C - 1:C, :]
        e_neg = jnp.exp(-c)
        e_end = jnp.exp(c_end - c)
        wide.append(dict(
            a=(-kn * jnp.exp(c - lw)).astype(BF16),
            b=(bn * e_neg).astype(BF16),
            k=(k * e_neg).astype(BF16),
            r=(r * jnp.exp(c)).astype(BF16),
            bh=(bn * e_end).astype(BF16),
            kh=(k * e_end).astype(BF16),
            v=v.astype(BF16),
            g_end=jnp.exp(c_end)))

    chains = [(b, h) for b in range(B) for h in range(RW_HEADS)]
    sl = lambda h: slice(h * N, (h + 1) * N)
    m4 = [_dot_nt(jnp.concatenate([wide[b]['a'][:, sl(h)], wide[b]['r'][:, sl(h)]], axis=0),
                  jnp.concatenate([wide[b]['b'][:, sl(h)], wide[b]['k'][:, sl(h)]], axis=0)) for b, h in chains]
    fill()
    abk = [jnp.where(strict, m[0:C, :], 0.0) for m in m4]
    rbk = [jnp.where(incl, m[C:2 * C, :], 0.0).astype(BF16) for m in m4]
    npow = [m[:, 0:C] for m in abk]
    tinv = [eye + n for n in npow]
    for _ in range(5):
        nb = [n.astype(BF16) for n in npow]
        npow = [_dot(n, n) for n in nb]
        fill()
        tinv = [t + _dot(t.astype(BF16), n.astype(BF16)) for t, n in zip(tinv, npow)]
        fill()
    s0 = [s_ref[b * RW_HEADS + h] for b, h in chains]
    s0b = [s.astype(BF16) for s in s0]
    x = [_dot_nt(wide[b]['a'][:, sl(h)], s) for (b, h), s in zip(chains, s0b)]
    fill()
    x = [xi + _dot(m[:, C:2 * C].astype(BF16), wide[b]['v'][:, sl(h)]) for (b, h), xi, m in zip(chains, x, abk)]
    fill()
    u = [_dot(t.astype(BF16), xi.astype(BF16)) for t, xi in zip(tinv, x)]
    fill()
    uv = [jnp.concatenate([ui.astype(BF16), wide[b]['v'][:, sl(h)]], axis=0) for (b, h), ui in zip(chains, u)]
    y = [_dot_nt(wide[b]['r'][:, sl(h)], s) for (b, h), s in zip(chains, s0b)]
    fill()
    y = [yi + _dot(m, w) for yi, m, w in zip(y, rbk, uv)]
    fill()
    snew = [_dot_tn(w, jnp.concatenate([wide[b]['bh'][:, sl(h)], wide[b]['kh'][:, sl(h)]], axis=0))
            for (b, h), w in zip(chains, uv)]
    fill()
    for (b, h), s, sn, yi in zip(chains, s0, snew, y):
        s_ref[b * RW_HEADS + h] = s * wide[b]['g_end'][:, sl(h)] + sn
        y_ref[b, :, sl(h)] = yi

    rows = lambda ref: jnp.concatenate([ref[b] for b in range(B)], axis=0)
    r, k, v, y = rows(r_ref), rows(k_ref), rows(v_ref), rows(y_ref)
    mean = _dot(y.astype(BF16), bd) * (1.0 / N)
    d = y - mean
    var = _dot((d * d).astype(BF16), bd) * (1.0 / N)
    yn = d * lax.rsqrt(var + RW_LN_EPS) * lnw_ref[...] + lnb_ref[...]
    bonus = _dot((r * k * rk_ref[...]).astype(BF16), bd)
    out = (yn + bonus * v) * rows(g_ref)
    for b in range(B):
        o_ref[b] = out[b * C:(b + 1) * C, :].astype(o_ref.dtype)


def _rwkv_scan_kernel(*refs):
    s_ref = refs[13]

    @pl.when(pl.program_id(0) == 0)
    def _():
        s_ref[...] = jnp.zeros_like(s_ref)

    _rwkv_chunk(*refs)


def _rwkv_scan(r, k, v, lw, kn, bn, g, ln_w, ln_b, r_k, tri, bd, B, S):
    T, W = r.shape
    C = RW_CHUNK
    blk = pl.BlockSpec((B, C, W), lambda c: (0, c, 0))
    full = lambda shape: pl.BlockSpec(shape, lambda c: (0,) * len(shape))
    seq = lambda t: t.reshape(B, S, W)
    out = pl.pallas_call(
        _rwkv_scan_kernel,
        out_shape=jax.ShapeDtypeStruct((B, S, W), BF16),
        grid=(S // C,),
        in_specs=[blk] * 7 + [full((1, W))] * 3 + [full((C, C)), full((W, W))],
        out_specs=blk,
        scratch_shapes=[pltpu.VMEM((B * RW_HEADS, RW_N, RW_N), F32), pltpu.VMEM((B, C, W), F32)],
        compiler_params=_cparams(("arbitrary",)),
        name="rwkv_scan",
    )(seq(r), seq(k), seq(v), seq(lw), seq(kn), seq(bn), seq(g), ln_w, ln_b, r_k, tri, bd)
    return out.reshape(T, W)


def _hgrn_kernel(q_ref, fz_ref, i_ref, og_ref, lb_ref, nw_ref, tri_ref, o_ref, st_ref, *, layer):
    @pl.when(pl.program_id(1) == 0)
    def _():
        st_ref[...] = jnp.zeros_like(st_ref)

    lbraw = lb_ref[...]
    e = jnp.exp(lbraw - jnp.max(lbraw, axis=0, keepdims=True))
    p = e / jnp.sum(e, axis=0, keepdims=True)
    lb = jnp.zeros((1, p.shape[1]), F32)
    for l in range(1, layer + 1):
        lb = lb + p[l:l + 1, :]
    fz = fz_ref[...]
    f = lb + (1.0 - lb) * jax.nn.sigmoid(fz)
    g = jnp.log(jnp.maximum(f, F_FLOOR))
    key = (1.0 - lb) * jax.nn.sigmoid(-fz)
    tri = tri_ref[...]
    g1 = g.astype(BF16)
    rem = g - g1.astype(F32)
    g2 = rem.astype(BF16)
    g3 = (rem - g2.astype(F32)).astype(BF16)
    bcum = _dot(tri, g1) + _dot(tri, g2) + _dot(tri, g3)
    bk = bcum - jnp.log(key)
    q = q_ref[...]
    v = i_ref[...]
    og = og_ref[...]
    qe = q * jnp.exp(bcum)
    rt = q.shape[0]
    n = HG_SUB
    rowi = lax.broadcasted_iota(jnp.int32, (n, HG_DK), 0)
    chunks = [slice(c * n, (c + 1) * n) for c in range(rt // n)]
    outs = []
    for h in range(HG_HEADS):
        hs = slice(h * HG_DK, (h + 1) * HG_DK)
        b_last = [bcum[rs, hs][n - 1:n, :] for rs in chunks]
        inc = [_dot_tn(v[rs, hs].astype(BF16), (key[rs, hs] * jnp.exp(bl - bcum[rs, hs])).astype(BF16))
               for rs, bl in zip(chunks, b_last)]
        intra = []
        for rs in chunks:
            bc, bks, qc, vc = bcum[rs, hs], bk[rs, hs], q[rs, hs], v[rs, hs]
            o = jnp.zeros((n, HG_DK), F32)
            for s in range(n):
                dec = jnp.exp(jnp.where(rowi >= s, bc - bks[s:s + 1, :], NEG_BIG))
                o = o + jnp.sum(qc * dec, axis=-1, keepdims=True) * vc[s:s + 1, :]
            intra.append(o)
        st = st_ref[h]
        starts = []
        for bl, d in zip(b_last, inc):
            starts.append(st)
            st = st * jnp.exp(bl) + d
        st_ref[h] = st
        inter = [_dot_nt(qe[rs, hs].astype(BF16), s0.astype(BF16)) for rs, s0 in zip(chunks, starts)]
        o = jnp.concatenate([a + b for a, b in zip(inter, intra)], axis=0)
        outs.append(_rms(o) * nw_ref[...] * (og[:, hs] * jax.nn.sigmoid(og[:, hs])))
    o_ref[...] = jnp.concatenate(outs, axis=1).astype(o_ref.dtype)


def _hgrn(u_hg, lb_raw, norm_w, tri, layer, B, S, rt):
    T = u_hg.shape[0]
    rt = min(rt, S)
    nR = S // rt
    L = lb_raw.shape[0]
    W = HG_WIDTH
    col = lambda j: pl.BlockSpec((rt, W), lambda b, i: (b * nR + i, U_HG // W + j))
    return pl.pallas_call(
        functools.partial(_hgrn_kernel, layer=layer),
        out_shape=jax.ShapeDtypeStruct((T, W), BF16),
        grid=(B, nR),
        in_specs=[
            col(0), col(1), col(2), col(3),
            pl.BlockSpec((L, W), lambda b, i: (0, 0)),
            pl.BlockSpec((1, HG_DK), lambda b, i: (0, 0)),
            pl.BlockSpec((rt, rt), lambda b, i: (0, 0)),
        ],
        out_specs=pl.BlockSpec((rt, W), lambda b, i: (b * nR + i, 0)),
        scratch_shapes=[pltpu.VMEM((HG_HEADS, HG_DK, HG_DK), F32)],
        compiler_params=_cparams(("arbitrary", "arbitrary")),
        name="hgrn",
    )(u_hg, u_hg, u_hg, u_hg, lb_raw, norm_w, tri)


def _mla_prep_kernel(cq_ref, ckv_ref, kr_ref, krs_ref, pos_ref, invf_ref, qnw_ref, kvnw_ref, wq_ref, wqs_ref, wk_ref,
                     wvt_ref, q_out, k_out, vt_out):
    cq = cq_ref[...]
    ckv = ckv_ref[...]
    kr = kr_ref[...]
    krs = krs_ref[...]
    ang = pos_ref[...].astype(F32) * invf_ref[...]
    lane = lax.broadcasted_iota(jnp.int32, ang.shape, 1)
    cosv = jnp.cos(ang)
    sinv = jnp.sin(ang)
    cos_t = jnp.where(lane < MLA_NOPE, 1.0, jnp.where(lane < MLA_NOPE + MLA_ROPE, cosv, 0.0))
    sin_t = jnp.where(lane < MLA_NOPE, 0.0,
                      jnp.where(lane < MLA_NOPE + MLA_ROPE // 2, -sinv,
                                jnp.where(lane < MLA_NOPE + MLA_ROPE, sinv, 0.0)))
    cqn = (_rms(cq) * qnw_ref[...]).astype(BF16)
    kvn = (_rms(ckv) * kvnw_ref[...]).astype(BF16)
    qm = _dot(cqn, wq_ref[...])
    qs = _dot(cqn, wqs_ref[...])
    kn = _dot(kvn, wk_ref[...])
    vt = _dot_nt(wvt_ref[...], kvn)
    k_rot = kr * cos_t + krs * sin_t
    for h in range(MLA_HEADS):
        sl = slice(h * MLA_PAD, (h + 1) * MLA_PAD)
        q_out[:, sl] = ((qm[:, sl] * cos_t + qs[:, sl] * sin_t) * MLA_EXP_SCALE).astype(BF16)
        k_out[:, sl] = (kn[:, sl] + k_rot).astype(BF16)
    vrow = lax.broadcasted_iota(jnp.int32, vt.shape, 0) % MLA_PAD
    vt_out[...] = jnp.where(vrow == MLA_V, 1.0, vt).astype(BF16)


def _mla_prep(u, pos, invf, qnw, kvnw, wq, wqs, wk, wvt, tm):
    T = u.shape[0]
    tm = min(tm, T)
    W = MLA_HEADS * MLA_PAD
    full = lambda shape: pl.BlockSpec(shape, lambda i: (0,) * len(shape))
    out_spec = pl.BlockSpec((tm, W), lambda i: (i, 0))
    ucol = lambda width, off: pl.BlockSpec((tm, width), lambda i: (i, off // width))
    o_kv = U_MLA + MLA_Q_RANK
    return pl.pallas_call(
        _mla_prep_kernel,
        out_shape=[jax.ShapeDtypeStruct((T, W), BF16)] * 2 + [jax.ShapeDtypeStruct((W, T), BF16)],
        grid=(T // tm,),
        in_specs=[
            ucol(MLA_Q_RANK, U_MLA), ucol(MLA_KV_RANK, o_kv), ucol(MLA_PAD, o_kv + MLA_KV_RANK),
            ucol(MLA_PAD, o_kv + MLA_KV_RANK + MLA_PAD),
            pl.BlockSpec((tm, 1), lambda i: (i, 0)),
            full((1, MLA_PAD)), full((1, MLA_Q_RANK)), full((1, MLA_KV_RANK)),
            full((MLA_Q_RANK, W)), full((MLA_Q_RANK, W)), full((MLA_KV_RANK, W)), full((W, MLA_KV_RANK)),
        ],
        out_specs=[out_spec] * 2 + [pl.BlockSpec((W, tm), lambda i: (0, i))],
        compiler_params=_cparams(("arbitrary",)),
        name="mla_prep",
    )(u, u, u, u, pos, invf, qnw, kvnw, wq, wqs, wk, wvt)


def _attn_kernel(qi_ref, ki_ref, q_ref, k_ref, vt_ref, o_ref, m_ref, acc_ref, *, groups):
    t = pl.program_id(2)
    qi = qi_ref[t]
    ki = ki_ref[t]
    blk = q_ref.shape[0]
    gw = blk // groups
    P = MLA_PAD

    @pl.when(ki == 0)
    def _():
        m_ref[...] = jnp.full_like(m_ref, NEG_BIG)
        acc_ref[...] = jnp.zeros_like(acc_ref)

    def step(diag):
        kv_len = lambda g: (g + 1) * gw if diag else blk
        hs = lambda h: slice(h * P, (h + 1) * P)

        def scores(h, g):
            return _dot_nt(k_ref[0:kv_len(g), hs(h)], q_ref[g * gw:(g + 1) * gw, hs(h)])

        def softmax_pv(h, g, s):
            if diag:
                row = lax.broadcasted_iota(jnp.int32, s.shape, 0)
                col = lax.broadcasted_iota(jnp.int32, s.shape, 1) + g * gw
                s = jnp.where(row <= col, s, NEG_BIG)
            cs = slice(g * gw, (g + 1) * gw)
            m_prev = m_ref[h, :, cs]
            m_new = jnp.maximum(m_prev, jnp.max(s, axis=0, keepdims=True))
            p = jnp.exp2(s - m_new).astype(BF16)
            acc_ref[h, :, cs] = (jnp.exp2(m_prev - m_new) * acc_ref[h, :, cs]
                                 + _dot(vt_ref[hs(h), 0:kv_len(g)], p))
            m_ref[h, :, cs] = m_new

        s0 = [scores(0, g) for g in range(groups)]
        s1 = []
        for g in range(groups):
            softmax_pv(0, g, s0[g])
            s1.append(scores(1, g))
        for g in range(groups):
            softmax_pv(1, g, s1[g])

    @pl.when(ki < qi)
    def _():
        step(False)

    @pl.when(ki == qi)
    def _():
        step(True)
        for h in range(2):
            acc = acc_ref[h]
            o_ref[:, h * P:(h + 1) * P] = (acc / acc[MLA_V:MLA_V + 1, :]).T.astype(o_ref.dtype)


def _mla_attn(q, k, vt, B, S, blk):
    T, W = q.shape
    blk = min(blk, S)
    nq = S // blk
    groups = max(1, min(8, blk // 128))
    qi = np.array([i for i in range(nq) for _ in range(i + 1)], np.int32)
    ki = np.array([j for i in range(nq) for j in range(i + 1)], np.int32)
    P2 = 2 * MLA_PAD
    grid_spec = pltpu.PrefetchScalarGridSpec(
        num_scalar_prefetch=2,
        grid=(B, MLA_HEADS // 2, len(qi)),
        in_specs=[
            pl.BlockSpec((blk, P2), lambda b, h, t, qi, ki: (b * nq + qi[t], h)),
            pl.BlockSpec((blk, P2), lambda b, h, t, qi, ki: (b * nq + ki[t], h)),
            pl.BlockSpec((P2, blk), lambda b, h, t, qi, ki: (h, b * nq + ki[t])),
        ],
        out_specs=pl.BlockSpec((blk, P2), lambda b, h, t, qi, ki: (b * nq + qi[t], h)),
        scratch_shapes=[pltpu.VMEM((2, 1, blk), F32), pltpu.VMEM((2, MLA_PAD, blk), F32)],
    )
    return pl.pallas_call(
        functools.partial(_attn_kernel, groups=groups),
        out_shape=jax.ShapeDtypeStruct((T, W), BF16),
        grid_spec=grid_spec,
        compiler_params=_cparams(("arbitrary", "arbitrary", "arbitrary")),
        name="mla_attn",
    )(jnp.asarray(qi), jnp.asarray(ki), q, k, vt)


def _merge_kernel(x_ref, gt_ref, oa_ref, ob_ref, oc_ref, pa_ref, pb_ref, pc_ref, wo_ref, g1_ref, o_ref):
    D = D_MODEL
    gt = gt_ref[...]
    y = gt[:, 0:D] * _dot(oa_ref[...], pa_ref[...])
    y = y + gt[:, D:2 * D] * _dot(ob_ref[...], pb_ref[...])
    y = y + gt[:, 2 * D:3 * D] * _dot(oc_ref[...], pc_ref[...])
    o_ref[...] = x_ref[...] + g1_ref[0] * _dot(y.astype(BF16), wo_ref[...])


def _merge(x, u_gate, o_a, o_b, o_c, pa, pb, pc, wo, g1, S, tm):
    T, D = x.shape
    tm = min(tm, S)
    rows = lambda w: pl.BlockSpec((tm, w), lambda i: (i, 0))
    full = lambda shape: pl.BlockSpec(shape, lambda i: (0,) * len(shape))
    return pl.pallas_call(
        _merge_kernel,
        out_shape=jax.ShapeDtypeStruct((T, D), F32),
        grid=(T // tm,),
        in_specs=[
            rows(D), rows(GATE_COLS), rows(RW_WIDTH), rows(HG_WIDTH), rows(MLA_HEADS * MLA_PAD),
            full(pa.shape), full(pb.shape), full(pc.shape), full(wo.shape),
            pl.BlockSpec((1, 1, D), lambda i: ((i * tm) // S, 0, 0)),
        ],
        out_specs=rows(D),
        compiler_params=_cparams(("arbitrary",)),
        name="merge",
    )(x, u_gate, o_a, o_b, o_c, pa, pb, pc, wo, g1)


def _ffn_up_kernel(x_ref, sh_ref, sc_ref, wv_ref, wg_ref, cwv_ref, cwg_ref, cbv_ref, cbg_ref, o_ref,
                   h_ref, carry_ref, *, blocks_per_seq):
    j = pl.program_id(1)

    @pl.when(j == 0)
    def _():
        h = _rms(x_ref[...]) * (1.0 + sc_ref[0]) + sh_ref[0]
        h_ref[...] = h.astype(BF16)

    @pl.when(pl.program_id(0) % blocks_per_seq == 0)
    def _():
        carry_ref[j] = jnp.zeros(carry_ref.shape[1:], F32)

    h = h_ref[...]
    tm = h.shape[0]
    carry = carry_ref[j]

    def conv(u, cw_ref, cb_ref, c2, c1):
        row = lax.broadcasted_iota(jnp.int32, (8, u.shape[1]), 0)
        r1 = pltpu.roll(u, 1, 0)
        r2 = pltpu.roll(u, 2, 0)
        u1 = jnp.concatenate([jnp.where(row == 0, c1, r1[0:8, :]), r1[8:, :]], axis=0)
        u2 = jnp.concatenate([jnp.where(row == 0, c2, jnp.where(row == 1, c1, r2[0:8, :])), r2[8:, :]], axis=0)
        cw = cw_ref[...]
        return cw[0:1, :] * u2 + cw[1:2, :] * u1 + cw[2:3, :] * u + cb_ref[...]

    uv = _dot(h, wv_ref[...])
    ug = _dot(h, wg_ref[...])
    val = conv(uv, cwv_ref, cbv_ref, carry[0:1, :], carry[1:2, :])
    gate = conv(ug, cwg_ref, cbg_ref, carry[2:3, :], carry[3:4, :])
    carry_ref[j] = jnp.concatenate([uv[tm - 2:tm, :], ug[tm - 2:tm, :]], axis=0)
    o_ref[...] = (gate * jax.nn.sigmoid(gate) * val).astype(o_ref.dtype)


def _ffn_up(x, shift, scale, wv, wg, cwv, cwg, cbv, cbg, S, tm, tn):
    T, D = x.shape
    tm = min(tm, S)
    ncol = D_FF // tn
    return pl.pallas_call(
        functools.partial(_ffn_up_kernel, blocks_per_seq=S // tm),
        out_shape=jax.ShapeDtypeStruct((T, D_FF), BF16),
        grid=(T // tm, ncol),
        in_specs=[
            pl.BlockSpec((tm, D), lambda i, j: (i, 0)),
            pl.BlockSpec((1, 1, D), lambda i, j: ((i * tm) // S, 0, 0)),
            pl.BlockSpec((1, 1, D), lambda i, j: ((i * tm) // S, 0, 0)),
            pl.BlockSpec((D, tn), lambda i, j: (0, j)),
            pl.BlockSpec((D, tn), lambda i, j: (0, j)),
            pl.BlockSpec((3, tn), lambda i, j: (0, j)),
            pl.BlockSpec((3, tn), lambda i, j: (0, j)),
            pl.BlockSpec((1, tn), lambda i, j: (0, j)),
            pl.BlockSpec((1, tn), lambda i, j: (0, j)),
        ],
        out_specs=pl.BlockSpec((tm, tn), lambda i, j: (i, j)),
        scratch_shapes=[pltpu.VMEM((tm, D), BF16), pltpu.VMEM((ncol, 4, tn), F32)],
        compiler_params=_cparams(("arbitrary", "arbitrary")),
        name="ffn_up",
    )(x, shift, scale, wv, wg, cwv, cwg, cbv, cbg)


def _ffn_down_kernel(x_ref, a_ref, w_ref, g2_ref, fw_ref, o_ref, *, final):
    y = x_ref[...] + g2_ref[0] * _dot(a_ref[...], w_ref[...])
    if final:
        y = _rms(y) * fw_ref[...]
    o_ref[...] = y


def _ffn_down(x, act, w_down, g2, final_w, S, tm, final):
    T, D = x.shape
    tm = min(tm, S)
    return pl.pallas_call(
        functools.partial(_ffn_down_kernel, final=final),
        out_shape=jax.ShapeDtypeStruct((T, D), F32),
        grid=(T // tm,),
        in_specs=[
            pl.BlockSpec((tm, D), lambda i: (i, 0)),
            pl.BlockSpec((tm, D_FF), lambda i: (i, 0)),
            pl.BlockSpec((D_FF, D), lambda i: (0, 0)),
            pl.BlockSpec((1, 1, D), lambda i: ((i * tm) // S, 0, 0)),
            pl.BlockSpec((1, D), lambda i: (0, 0)),
        ],
        out_specs=pl.BlockSpec((tm, D), lambda i: (i, 0)),
        compiler_params=_cparams(("arbitrary",)),
        name="ffn_down",
    )(x, act, w_down, g2, final_w)


def _tiles(S):
    return dict(inproj=min(1024, S), rwkv_prep=min(512, S), hgrn=min(256, S), mla_prep=min(512, S),
                attn=min(2048, S), merge=min(512, S), ffn=min(512, S))


def _head_pad(w, per_head, lo, hi, dst, pad=MLA_PAD):
    K = w.shape[0]
    w = w.reshape(K, MLA_HEADS, per_head)[:, :, lo:hi]
    out = jnp.zeros((K, MLA_HEADS, pad), w.dtype)
    out = out.at[:, :, dst:dst + (hi - lo)].set(w)
    return out.reshape(K, MLA_HEADS * pad)


def _swap_halves(w):
    half = w.shape[-1] // 2
    return jnp.concatenate([w[..., half:], w[..., :half]], axis=-1)


def kernel(x, c, positions, ada_w, ada_b, w_in, rwkv_mu, rwkv_w0, rwkv_w2, rwkv_a0, rwkv_a2, rwkv_g2, rwkv_k_k, rwkv_k_a, rwkv_r_k, rwkv_ln_w, rwkv_ln_b, hgrn_lb, hgrn_norm_w, mla_q_norm_w, mla_w_uq, mla_kv_norm_w, mla_w_ukv, branch_proj_a, branch_proj_b, branch_proj_c, w_out, ffn_w_up, ffn_conv_w, ffn_conv_b, ffn_w_down, final_norm_w):
    B, S, D = x.shape
    L = ada_w.shape[0]
    T = B * S
    xt = x.reshape(T, D)
    pos = positions.reshape(T, 1)

    mod = _adaln(c, ada_w, ada_b).reshape(L, B, 6, 1, D)

    inv_freq = ROPE_THETA ** (-jnp.arange(0, MLA_ROPE, 2, dtype=F32) / MLA_ROPE)
    invf = jnp.zeros((1, MLA_PAD), F32).at[0, MLA_NOPE:MLA_NOPE + MLA_ROPE].set(jnp.concatenate([inv_freq, inv_freq]))

    ii = np.arange(RW_WIDTH)
    bd = jnp.asarray((ii[:, None] // RW_N == ii[None, :] // RW_N).astype(np.float32))
    cc = np.arange(RW_CHUNK)
    tri_rw = jnp.asarray((cc[:, None] >= cc[None, :]).astype(np.float32)).astype(BF16)
    bd_b = bd.astype(BF16)
    tl = _tiles(S)
    rt_hg = tl['hgrn']
    rr = np.arange(rt_hg)
    tri_hg = jnp.asarray(((rr[:, None] >= rr[None, :])
                          & (rr[:, None] // HG_SUB == rr[None, :] // HG_SUB)).astype(np.float32)).astype(BF16)

    o0 = RW_COLS
    o1 = o0 + HG_COLS
    o2 = o1 + MLA_Q_RANK + MLA_KV_RANK + MLA_ROPE
    for l in range(L):
        sh1, sc1, g1, sh2, sc2, g2 = (mod[l, :, j] for j in range(6))
        wl = w_in[l]
        w_gate = wl[:, o2:o2 + GATE_COLS].astype(BF16)
        w_kr = wl[:, o2 - MLA_ROPE:o2]
        lanes = jnp.zeros((D, MLA_PAD), F32)
        w_u = jnp.concatenate([
            wl[:, o0:o1],
            wl[:, 0:o0],
            wl[:, o1:o1 + MLA_Q_RANK + MLA_KV_RANK],
            lanes.at[:, MLA_NOPE:MLA_NOPE + MLA_ROPE].set(w_kr),
            lanes.at[:, MLA_NOPE:MLA_NOPE + MLA_ROPE].set(_swap_halves(w_kr)),
            jnp.zeros((D, U_COLS - U_MLA - MLA_SEG), F32),
        ], axis=1).astype(BF16)

        u = _modmm(xt, sh1, sc1, w_u, S, tl['inproj'], U_COLS // 2, "inproj")
        gates = _modmm(xt, sh1, sc1, w_gate, S, tl['inproj'], GATE_COLS // 2, "inproj_gate", gate=True)

        zpad = jnp.zeros((64, RW_WIDTH), F32)
        r, k2, v, lw, kn, bn, g = _rwkv_prep(
            u, rwkv_mu[l].reshape(1, -1), rwkv_w0[l].reshape(1, -1),
            jnp.concatenate([rwkv_w2[l], zpad], axis=0).astype(BF16), rwkv_a0[l].reshape(1, -1),
            jnp.concatenate([zpad, rwkv_a2[l]], axis=0).astype(BF16), rwkv_g2[l].astype(BF16),
            rwkv_k_k[l].reshape(1, -1), rwkv_k_a[l].reshape(1, -1), bd_b, S, tl['rwkv_prep'])
        o_a = _rwkv_scan(r, k2, v, lw, kn, bn, g, rwkv_ln_w[l].reshape(1, -1), rwkv_ln_b[l].reshape(1, -1),
                         rwkv_r_k[l].reshape(1, -1), tri_rw, bd_b, B, S)

        o_b = _hgrn(u, hgrn_lb, hgrn_norm_w[l].reshape(1, -1), tri_hg, l, B, S, rt_hg)

        per_q = MLA_NOPE + MLA_ROPE
        wq = _head_pad(mla_w_uq[l], per_q, 0, per_q, 0).astype(BF16)
        wq_rope = mla_w_uq[l].reshape(MLA_Q_RANK, MLA_HEADS, per_q)[:, :, MLA_NOPE:]
        wqs = _head_pad(_swap_halves(wq_rope).reshape(MLA_Q_RANK, -1), MLA_ROPE, 0, MLA_ROPE, MLA_NOPE).astype(BF16)
        per_kv = MLA_NOPE + MLA_V
        wk = _head_pad(mla_w_ukv[l], per_kv, 0, MLA_NOPE, 0).astype(BF16)
        wvt = _head_pad(mla_w_ukv[l], per_kv, MLA_NOPE, per_kv, 0).T.astype(BF16)
        q_cat, k_cat, v_t = _mla_prep(u, pos, invf, mla_q_norm_w[l].reshape(1, -1),
                                      mla_kv_norm_w[l].reshape(1, -1), wq, wqs, wk, wvt, tl['mla_prep'])
        o_c = _mla_attn(q_cat, k_cat, v_t, B, S, tl['attn'])

        pc = jnp.zeros((MLA_HEADS, MLA_PAD, D), F32).at[:, :MLA_V, :].set(
            branch_proj_c[l].reshape(MLA_HEADS, MLA_V, D)).reshape(MLA_HEADS * MLA_PAD, D).astype(BF16)
        xt = _merge(xt, gates, o_a, o_b, o_c, branch_proj_a[l].astype(BF16), branch_proj_b[l].astype(BF16), pc,
                    w_out[l].astype(BF16), g1, S, tl['merge'])

        wu = ffn_w_up[l]
        cw = ffn_conv_w[l]
        cb = ffn_conv_b[l].reshape(1, -1)
        act = _ffn_up(xt, sh2, sc2, wu[:, :D_FF].astype(BF16), wu[:, D_FF:].astype(BF16),
                      cw[:, :D_FF], cw[:, D_FF:], cb[:, :D_FF], cb[:, D_FF:], S, tl['ffn'], D_FF)
        xt = _ffn_down(xt, act, ffn_w_down[l].astype(BF16), g2, final_norm_w.reshape(1, -1), S, tl['ffn'], l == L - 1)

    return xt.reshape(B, S, D)
```

```python
import functools
import math

import numpy as np
import jax
import jax.numpy as jnp
from jax import lax
from jax.experimental import pallas as pl
from jax.experimental.pallas import tpu as pltpu

F32 = jnp.float32
BF16 = jnp.bfloat16
HI = lax.Precision.HIGHEST

D_MODEL = 1024
NORM_EPS = 1e-6

RW_HEADS = 8
RW_N = 64
RW_WIDTH = RW_HEADS * RW_N
RW_COLS = 1792
RW_LN_EPS = 64e-5
RW_CHUNK = 64

HG_HEADS = 4
HG_DK = 128
HG_WIDTH = HG_HEADS * HG_DK
HG_COLS = 2048
HG_SUB = 16
F_FLOOR = 1e-30

MLA_HEADS = 8
MLA_Q_RANK = 256
MLA_KV_RANK = 128
MLA_NOPE = 64
MLA_ROPE = 32
MLA_V = 64
MLA_PAD = 128
MLA_SEG = 640
MLA_SCALE = (MLA_NOPE + MLA_ROPE) ** -0.5
MLA_EXP_SCALE = MLA_SCALE * math.log2(math.e)
ROPE_THETA = 10000.0
NEG_BIG = -1e30

D_FF = 2816
GATE_COLS = 3 * D_MODEL

U_HG = 0
U_RW = U_HG + HG_COLS
U_MLA = U_RW + RW_COLS
U_COLS = 4608

VMEM_LIMIT = 56 * 1024 * 1024


def _cparams(sem):
    return pltpu.CompilerParams(dimension_semantics=sem, vmem_limit_bytes=VMEM_LIMIT)


def _dot(a, b, precision=None):
    return jnp.dot(a, b, precision=precision, preferred_element_type=F32)


def _dot_nt(a, b, precision=None):
    return lax.dot_general(a, b, (((1,), (1,)), ((), ())), precision=precision, preferred_element_type=F32)


def _dot_tn(a, b, precision=None):
    return lax.dot_general(a, b, (((0,), (0,)), ((), ())), precision=precision, preferred_element_type=F32)


def _rms(x):
    return x * lax.rsqrt(jnp.mean(x * x, axis=-1, keepdims=True) + NORM_EPS)


def _adaln_kernel(c_ref, w_ref, b_ref, o_ref):
    c = c_ref[...]
    cond = c * jax.nn.sigmoid(c)
    o_ref[0] = _dot(cond, w_ref[0], HI) + b_ref[0]


def _adaln(c, ada_w, ada_b):
    L, D, N = ada_w.shape
    B = c.shape[0]
    tn = 1536
    return pl.pallas_call(
        _adaln_kernel,
        out_shape=jax.ShapeDtypeStruct((L, B, N), F32),
        grid=(L, N // tn),
        in_specs=[
            pl.BlockSpec((B, D), lambda l, j: (0, 0)),
            pl.BlockSpec((1, D, tn), lambda l, j: (l, 0, j)),
            pl.BlockSpec((1, 1, tn), lambda l, j: (l, 0, j)),
        ],
        out_specs=pl.BlockSpec((1, B, tn), lambda l, j: (l, 0, j)),
        compiler_params=_cparams(("arbitrary", "arbitrary")),
        name="adaln",
    )(c, ada_w, ada_b.reshape(L, 1, N))


def _modmm_kernel(x_ref, sh_ref, sc_ref, w_ref, o_ref, h_ref, *, gate):
    @pl.when(pl.program_id(1) == 0)
    def _():
        h = _rms(x_ref[...]) * (1.0 + sc_ref[0]) + sh_ref[0]
        h_ref[...] = h.astype(BF16)

    u = _dot(h_ref[...], w_ref[...])
    o_ref[...] = (jax.nn.sigmoid(u) if gate else u).astype(o_ref.dtype)


def _modmm(x, shift, scale, w, S, tm, tn, name, gate=False):
    T, D = x.shape
    N = w.shape[1]
    tm = min(tm, S)
    return pl.pallas_call(
        functools.partial(_modmm_kernel, gate=gate),
        out_shape=jax.ShapeDtypeStruct((T, N), BF16 if gate else F32),
        grid=(T // tm, N // tn),
        in_specs=[
            pl.BlockSpec((tm, D), lambda i, j: (i, 0)),
            pl.BlockSpec((1, 1, D), lambda i, j: ((i * tm) // S, 0, 0)),
            pl.BlockSpec((1, 1, D), lambda i, j: ((i * tm) // S, 0, 0)),
            pl.BlockSpec((D, tn), lambda i, j: (0, j)),
        ],
        out_specs=pl.BlockSpec((tm, tn), lambda i, j: (i, j)),
        scratch_shapes=[pltpu.VMEM((tm, D), BF16)],
        compiler_params=_cparams(("arbitrary", "arbitrary")),
        name=name,
    )(x, shift, scale, w)


def _rwkv_prep_kernel(ur_ref, uk_ref, uv_ref, ul_ref, mu_ref, w0_ref, w2_ref, a0_ref, a2_ref, g2_ref, kk_ref, ka_ref,
                      bd_ref, r_out, k_out, v_out, lw_out, kn_out, bn_out, g_out, carry_ref, *, blocks_per_seq):
    @pl.when(pl.program_id(0) % blocks_per_seq == 0)
    def _():
        carry_ref[...] = jnp.zeros_like(carry_ref)

    u = jnp.concatenate([ur_ref[...], uk_ref[...], uv_ref[...], ul_ref[...]], axis=1)
    tm = u.shape[0]
    row = lax.broadcasted_iota(jnp.int32, u.shape, 0)
    prev = jnp.where(row == 0, carry_ref[...], pltpu.roll(u, 1, 0))
    carry_ref[...] = u[tm - 1:tm, :]
    xm = u + (prev - u) * mu_ref[...]
    W = RW_WIDTH
    r = xm[:, 0:W]
    k = xm[:, W:2 * W]
    v = xm[:, 2 * W:3 * W]
    la = xm[:, 3 * W:3 * W + 128]
    gl = xm[:, 3 * W + 128:3 * W + 256]
    w = -jax.nn.softplus(-(w0_ref[...] + _dot(jnp.tanh(la).astype(BF16), w2_ref[...]))) - 0.5
    lw = -jnp.exp(w)
    a = jax.nn.sigmoid(a0_ref[...] + _dot(la.astype(BF16), a2_ref[...]))
    g = _dot(jax.nn.sigmoid(gl).astype(BF16), g2_ref[...])
    kk = k * kk_ref[...]
    k2hi = (kk * kk).astype(BF16)
    k2lo = (kk * kk - k2hi.astype(F32)).astype(BF16)
    bd = bd_ref[...]
    ss = _dot(k2hi, bd) + _dot(k2lo, bd)
    kn = kk / jnp.maximum(jnp.sqrt(ss), 1e-12)
    k2 = k * (1.0 + (a - 1.0) * ka_ref[...])
    r_out[...] = r
    k_out[...] = k2
    v_out[...] = v
    lw_out[...] = lw
    kn_out[...] = kn
    bn_out[...] = kn * a
    g_out[...] = g


def _rwkv_prep(u, mu, w0, w2p, a0, a2p, g2, k_k, k_a, bd, S, tm):
    T = u.shape[0]
    tm = min(tm, S)
    W = RW_WIDTH
    full = lambda shape: pl.BlockSpec(shape, lambda i: (0,) * len(shape))
    row_spec = pl.BlockSpec((tm, W), lambda i: (i, 0))
    ucol = lambda width, off: pl.BlockSpec((tm, width), lambda i: (i, off // width))
    return pl.pallas_call(
        functools.partial(_rwkv_prep_kernel, blocks_per_seq=S // tm),
        out_shape=[jax.ShapeDtypeStruct((T, W), F32)] * 7,
        grid=(T // tm,),
        in_specs=[
            ucol(W, U_RW), ucol(W, U_RW + W), ucol(W, U_RW + 2 * W), ucol(256, U_RW + 3 * W),
            full((1, RW_COLS)), full((1, W)), full((128, W)), full((1, W)), full((128, W)), full((128, W)),
            full((1, W)), full((1, W)), full((W, W)),
        ],
        out_specs=[row_spec] * 7,
        scratch_shapes=[pltpu.VMEM((1, RW_COLS), F32)],
        compiler_params=_cparams(("arbitrary",)),
        name="rwkv_prep",
    )(u, u, u, u, mu, w0, w2p, a0, a2p, g2, k_k, k_a, bd)


def _rwkv_chunk(r_ref, k_ref, v_ref, lw_ref, kn_ref, bn_ref, g_ref, lnw_ref, lnb_ref, rk_ref, tri_ref, bd_ref,
                o_ref, s_ref, y_ref, fill=lambda: None):
    C = RW_CHUNK
    N = RW_N
    B = r_ref.shape[0]
    tri = tri_ref[...]
    bd = bd_ref[...]
    row = lax.broadcasted_iota(jnp.int32, (C, 2 * C), 0)
    col = lax.broadcasted_iota(jnp.int32, (C, 2 * C), 1) % C
    strict = row > col
    incl = row >= col
    eye = (lax.broadcasted_iota(jnp.int32, (C, C), 0) == lax.broadcasted_iota(jnp.int32, (C, C), 1)).astype(F32)

    wide = []
    for b in range(B):
        r = r_ref[b]
        k = k_ref[b]
        v = v_ref[b]
        lw = lw_ref[b]
        kn = kn_ref[b]
        bn = bn_ref[b]
        lw1 = lw.astype(BF16)
        rem = lw - lw1.astype(F32)
        lw2 = rem.astype(BF16)
        lw3 = (rem - lw2.astype(F32)).astype(BF16)
        c = _dot(tri, lw1) + _dot(tri, lw2) + _dot(tri, lw3)
        c_end = c[C - 1:C, :]
        e_neg = jnp.exp(-c)
        e_end = jnp.exp(c_end - c)
        wide.append(dict(
            a=(-kn * jnp.exp(c - lw)).astype(BF16),
            b=(bn * e_neg).astype(BF16),
            k=(k * e_neg).astype(BF16),
            r=(r * jnp.exp(c)).astype(BF16),
            bh=(bn * e_end).astype(BF16),
            kh=(k * e_end).astype(BF16),
            v=v.astype(BF16),
            g_end=jnp.exp(c_end)))

    chains = [(b, h) for b in range(B) for h in range(RW_HEADS)]
    sl = lambda h: slice(h * N, (h + 1) * N)
    m4 = [_dot_nt(jnp.concatenate([wide[b]['a'][:, sl(h)], wide[b]['r'][:, sl(h)]], axis=0),
                  jnp.concatenate([wide[b]['b'][:, sl(h)], wide[b]['k'][:, sl(h)]], axis=0)) for b, h in chains]
    fill()
    abk = [jnp.where(strict, m[0:C, :], 0.0) for m in m4]
    rbk = [jnp.where(incl, m[C:2 * C, :], 0.0).astype(BF16) for m in m4]
    npow = [m[:, 0:C] for m in abk]
    tinv = [eye + n for n in npow]
    for _ in range(5):
        nb = [n.astype(BF16) for n in npow]
        npow = [_dot(n, n) for n in nb]
        fill()
        tinv = [t + _dot(t.astype(BF16), n.astype(BF16)) for t, n in zip(tinv, npow)]
        fill()
    s0 = [s_ref[b * RW_HEADS + h] for b, h in chains]
    s0b = [s.astype(BF16) for s in s0]
    x = [_dot_nt(wide[b]['a'][:, sl(h)], s) for (b, h), s in zip(chains, s0b)]
    fill()
    x = [xi + _dot(m[:, C:2 * C].astype(BF16), wide[b]['v'][:, sl(h)]) for (b, h), xi, m in zip(chains, x, abk)]
    fill()
    u = [_dot(t.astype(BF16), xi.astype(BF16)) for t, xi in zip(tinv, x)]
    fill()
    uv = [jnp.concatenate([ui.astype(BF16), wide[b]['v'][:, sl(h)]], axis=0) for (b, h), ui in zip(chains, u)]
    y = [_dot_nt(wide[b]['r'][:, sl(h)], s) for (b, h), s in zip(chains, s0b)]
    fill()
    y = [yi + _dot(m, w) for yi, m, w in zip(y, rbk, uv)]
    fill()
    snew = [_dot_tn(w, jnp.concatenate([wide[b]['bh'][:, sl(h)], wide[b]['kh'][:, sl(h)]], axis=0))
            for (b, h), w in zip(chains, uv)]
    fill()
    for (b, h), s, sn, yi in zip(chains, s0, snew, y):
        s_ref[b * RW_HEADS + h] = s * wide[b]['g_end'][:, sl(h)] + sn
        y_ref[b, :, sl(h)] = yi

    rows = lambda ref: jnp.concatenate([ref[b] for b in range(B)], axis=0)
    r, k, v, y = rows(r_ref), rows(k_ref), rows(v_ref), rows(y_ref)
    mean = _dot(y.astype(BF16), bd) * (1.0 / N)
    d = y - mean
    var = _dot((d * d).astype(BF16), bd) * (1.0 / N)
    yn = d * lax.rsqrt(var + RW_LN_EPS) * lnw_ref[...] + lnb_ref[...]
    bonus = _dot((r * k * rk_ref[...]).astype(BF16), bd)
    out = (yn + bonus * v) * rows(g_ref)
    for b in range(B):
        o_ref[b] = out[b * C:(b + 1) * C, :].astype(o_ref.dtype)


def _rwkv_scan_kernel(*refs):
    s_ref = refs[13]

    @pl.when(pl.program_id(0) == 0)
    def _():
        s_ref[...] = jnp.zeros_like(s_ref)

    _rwkv_chunk(*refs)


def _rwkv_scan(r, k, v, lw, kn, bn, g, ln_w, ln_b, r_k, tri, bd, B, S):
    T, W = r.shape
    C = RW_CHUNK
    blk = pl.BlockSpec((B, C, W), lambda c: (0, c, 0))
    full = lambda shape: pl.BlockSpec(shape, lambda c: (0,) * len(shape))
    seq = lambda t: t.reshape(B, S, W)
    out = pl.pallas_call(
        _rwkv_scan_kernel,
        out_shape=jax.ShapeDtypeStruct((B, S, W), BF16),
        grid=(S // C,),
        in_specs=[blk] * 7 + [full((1, W))] * 3 + [full((C, C)), full((W, W))],
        out_specs=blk,
        scratch_shapes=[pltpu.VMEM((B * RW_HEADS, RW_N, RW_N), F32), pltpu.VMEM((B, C, W), F32)],
        compiler_params=_cparams(("arbitrary",)),
        name="rwkv_scan",
    )(seq(r), seq(k), seq(v), seq(lw), seq(kn), seq(bn), seq(g), ln_w, ln_b, r_k, tri, bd)
    return out.reshape(T, W)


def _hgrn_kernel(q_ref, fz_ref, i_ref, og_ref, lb_ref, nw_ref, tri_ref, o_ref, st_ref, *, layer):
    @pl.when(pl.program_id(1) == 0)
    def _():
        st_ref[...] = jnp.zeros_like(st_ref)

    lbraw = lb_ref[...]
    e = jnp.exp(lbraw - jnp.max(lbraw, axis=0, keepdims=True))
    p = e / jnp.sum(e, axis=0, keepdims=True)
    lb = jnp.zeros((1, p.shape[1]), F32)
    for l in range(1, layer + 1):
        lb = lb + p[l:l + 1, :]
    fz = fz_ref[...]
    f = lb + (1.0 - lb) * jax.nn.sigmoid(fz)
    g = jnp.log(jnp.maximum(f, F_FLOOR))
    key = (1.0 - lb) * jax.nn.sigmoid(-fz)
    tri = tri_ref[...]
    g1 = g.astype(BF16)
    rem = g - g1.astype(F32)
    g2 = rem.astype(BF16)
    g3 = (rem - g2.astype(F32)).astype(BF16)
    bcum = _dot(tri, g1) + _dot(tri, g2) + _dot(tri, g3)
    bk = bcum - jnp.log(key)
    q = q_ref[...]
    v = i_ref[...]
    og = og_ref[...]
    qe = q * jnp.exp(bcum)
    rt = q.shape[0]
    n = HG_SUB
    rowi = lax.broadcasted_iota(jnp.int32, (n, HG_DK), 0)
    chunks = [slice(c * n, (c + 1) * n) for c in range(rt // n)]
    outs = []
    for h in range(HG_HEADS):
        hs = slice(h * HG_DK, (h + 1) * HG_DK)
        b_last = [bcum[rs, hs][n - 1:n, :] for rs in chunks]
        inc = [_dot_tn(v[rs, hs].astype(BF16), (key[rs, hs] * jnp.exp(bl - bcum[rs, hs])).astype(BF16))
               for rs, bl in zip(chunks, b_last)]
        intra = []
        for rs in chunks:
            bc, bks, qc, vc = bcum[rs, hs], bk[rs, hs], q[rs, hs], v[rs, hs]
            o = jnp.zeros((n, HG_DK), F32)
            for s in range(n):
                dec = jnp.exp(jnp.where(rowi >= s, bc - bks[s:s + 1, :], NEG_BIG))
                o = o + jnp.sum(qc * dec, axis=-1, keepdims=True) * vc[s:s + 1, :]
            intra.append(o)
        st = st_ref[h]
        starts = []
        for bl, d in zip(b_last, inc):
            starts.append(st)
            st = st * jnp.exp(bl) + d
        st_ref[h] = st
        inter = [_dot_nt(qe[rs, hs].astype(BF16), s0.astype(BF16)) for rs, s0 in zip(chunks, starts)]
        o = jnp.concatenate([a + b for a, b in zip(inter, intra)], axis=0)
        outs.append(_rms(o) * nw_ref[...] * (og[:, hs] * jax.nn.sigmoid(og[:, hs])))
    o_ref[...] = jnp.concatenate(outs, axis=1).astype(o_ref.dtype)


def _hgrn(u_hg, lb_raw, norm_w, tri, layer, B, S, rt):
    T = u_hg.shape[0]
    rt = min(rt, S)
    nR = S // rt
    L = lb_raw.shape[0]
    W = HG_WIDTH
    col = lambda j: pl.BlockSpec((rt, W), lambda b, i: (b * nR + i, U_HG // W + j))
    return pl.pallas_call(
        functools.partial(_hgrn_kernel, layer=layer),
        out_shape=jax.ShapeDtypeStruct((T, W), BF16),
        grid=(B, nR),
        in_specs=[
            col(0), col(1), col(2), col(3),
            pl.BlockSpec((L, W), lambda b, i: (0, 0)),
            pl.BlockSpec((1, HG_DK), lambda b, i: (0, 0)),
            pl.BlockSpec((rt, rt), lambda b, i: (0, 0)),
        ],
        out_specs=pl.BlockSpec((rt, W), lambda b, i: (b * nR + i, 0)),
        scratch_shapes=[pltpu.VMEM((HG_HEADS, HG_DK, HG_DK), F32)],
        compiler_params=_cparams(("arbitrary", "arbitrary")),
        name="hgrn",
    )(u_hg, u_hg, u_hg, u_hg, lb_raw, norm_w, tri)


def _mla_prep_kernel(cq_ref, ckv_ref, kr_ref, krs_ref, pos_ref, invf_ref, qnw_ref, kvnw_ref, wq_ref, wqs_ref, wk_ref,
                     wvt_ref, q_out, k_out, vt_out):
    cq = cq_ref[...]
    ckv = ckv_ref[...]
    kr = kr_ref[...]
    krs = krs_ref[...]
    ang = pos_ref[...].astype(F32) * invf_ref[...]
    lane = lax.broadcasted_iota(jnp.int32, ang.shape, 1)
    cosv = jnp.cos(ang)
    sinv = jnp.sin(ang)
    cos_t = jnp.where(lane < MLA_NOPE, 1.0, jnp.where(lane < MLA_NOPE + MLA_ROPE, cosv, 0.0))
    sin_t = jnp.where(lane < MLA_NOPE, 0.0,
                      jnp.where(lane < MLA_NOPE + MLA_ROPE // 2, -sinv,
                                jnp.where(lane < MLA_NOPE + MLA_ROPE, sinv, 0.0)))
    cqn = (_rms(cq) * qnw_ref[...]).astype(BF16)
    kvn = (_rms(ckv) * kvnw_ref[...]).astype(BF16)
    qm = _dot(cqn, wq_ref[...])
    qs = _dot(cqn, wqs_ref[...])
    kn = _dot(kvn, wk_ref[...])
    vt = _dot_nt(wvt_ref[...], kvn)
    k_rot = kr * cos_t + krs * sin_t
    for h in range(MLA_HEADS):
        sl = slice(h * MLA_PAD, (h + 1) * MLA_PAD)
        q_out[:, sl] = ((qm[:, sl] * cos_t + qs[:, sl] * sin_t) * MLA_EXP_SCALE).astype(BF16)
        k_out[:, sl] = (kn[:, sl] + k_rot).astype(BF16)
    vrow = lax.broadcasted_iota(jnp.int32, vt.shape, 0) % MLA_PAD
    vt_out[...] = jnp.where(vrow == MLA_V, 1.0, vt).astype(BF16)


def _mla_prep(u, pos, invf, qnw, kvnw, wq, wqs, wk, wvt, tm):
    T = u.shape[0]
    tm = min(tm, T)
    W = MLA_HEADS * MLA_PAD
    full = lambda shape: pl.BlockSpec(shape, lambda i: (0,) * len(shape))
    out_spec = pl.BlockSpec((tm, W), lambda i: (i, 0))
    ucol = lambda width, off: pl.BlockSpec((tm, width), lambda i: (i, off // width))
    o_kv = U_MLA + MLA_Q_RANK
    return pl.pallas_call(
        _mla_prep_kernel,
        out_shape=[jax.ShapeDtypeStruct((T, W), BF16)] * 2 + [jax.ShapeDtypeStruct((W, T), BF16)],
        grid=(T // tm,),
        in_specs=[
            ucol(MLA_Q_RANK, U_MLA), ucol(MLA_KV_RANK, o_kv), ucol(MLA_PAD, o_kv + MLA_KV_RANK),
            ucol(MLA_PAD, o_kv + MLA_KV_RANK + MLA_PAD),
            pl.BlockSpec((tm, 1), lambda i: (i, 0)),
            full((1, MLA_PAD)), full((1, MLA_Q_RANK)), full((1, MLA_KV_RANK)),
            full((MLA_Q_RANK, W)), full((MLA_Q_RANK, W)), full((MLA_KV_RANK, W)), full((W, MLA_KV_RANK)),
        ],
        out_specs=[out_spec] * 2 + [pl.BlockSpec((W, tm), lambda i: (0, i))],
        compiler_params=_cparams(("arbitrary",)),
        name="mla_prep",
    )(u, u, u, u, pos, invf, qnw, kvnw, wq, wqs, wk, wvt)


def _attn_kernel(qi_ref, ki_ref, q_ref, k_ref, vt_ref, o_ref, m_ref, acc_ref, *, groups):
    t = pl.program_id(2)
    qi = qi_ref[t]
    ki = ki_ref[t]
    blk = q_ref.shape[0]
    gw = blk // groups
    P = MLA_PAD

    @pl.when(ki == 0)
    def _():
        m_ref[...] = jnp.full_like(m_ref, NEG_BIG)
        acc_ref[...] = jnp.zeros_like(acc_ref)

    def step(diag):
        kv_len = lambda g: (g + 1) * gw if diag else blk
        hs = lambda h: slice(h * P, (h + 1) * P)

        def scores(h, g):
            return _dot_nt(k_ref[0:kv_len(g), hs(h)], q_ref[g * gw:(g + 1) * gw, hs(h)])

        def softmax_pv(h, g, s):
            if diag:
                row = lax.broadcasted_iota(jnp.int32, s.shape, 0)
                col = lax.broadcasted_iota(jnp.int32, s.shape, 1) + g * gw
                s = jnp.where(row <= col, s, NEG_BIG)
            cs = slice(g * gw, (g + 1) * gw)
            m_prev = m_ref[h, :, cs]
            m_new = jnp.maximum(m_prev, jnp.max(s, axis=0, keepdims=True))
            p = jnp.exp2(s - m_new).astype(BF16)
            acc_ref[h, :, cs] = (jnp.exp2(m_prev - m_new) * acc_ref[h, :, cs]
                                 + _dot(vt_ref[hs(h), 0:kv_len(g)], p))
            m_ref[h, :, cs] = m_new

        s0 = [scores(0, g) for g in range(groups)]
        s1 = []
        for g in range(groups):
            softmax_pv(0, g, s0[g])
            s1.append(scores(1, g))
        for g in range(groups):
            softmax_pv(1, g, s1[g])

    @pl.when(ki < qi)
    def _():
        step(False)

    @pl.when(ki == qi)
    def _():
        step(True)
        for h in range(2):
            acc = acc_ref[h]
            o_ref[:, h * P:(h + 1) * P] = (acc / acc[MLA_V:MLA_V + 1, :]).T.astype(o_ref.dtype)


def _mla_attn(q, k, vt, B, S, blk):
    T, W = q.shape
    blk = min(blk, S)
    nq = S // blk
    groups = max(1, min(8, blk // 128))
    qi = np.array([i for i in range(nq) for _ in range(i + 1)], np.int32)
    ki = np.array([j for i in range(nq) for j in range(i + 1)], np.int32)
    P2 = 2 * MLA_PAD
    grid_spec = pltpu.PrefetchScalarGridSpec(
        num_scalar_prefetch=2,
        grid=(B, MLA_HEADS // 2, len(qi)),
        in_specs=[
            pl.BlockSpec((blk, P2), lambda b, h, t, qi, ki: (b * nq + qi[t], h)),
            pl.BlockSpec((blk, P2), lambda b, h, t, qi, ki: (b * nq + ki[t], h)),
            pl.BlockSpec((P2, blk), lambda b, h, t, qi, ki: (h, b * nq + ki[t])),
        ],
        out_specs=pl.BlockSpec((blk, P2), lambda b, h, t, qi, ki: (b * nq + qi[t], h)),
        scratch_shapes=[pltpu.VMEM((2, 1, blk), F32), pltpu.VMEM((2, MLA_PAD, blk), F32)],
    )
    return pl.pallas_call(
        functools.partial(_attn_kernel, groups=groups),
        out_shape=jax.ShapeDtypeStruct((T, W), BF16),
        grid_spec=grid_spec,
        compiler_params=_cparams(("arbitrary", "arbitrary", "arbitrary")),
        name="mla_attn",
    )(jnp.asarray(qi), jnp.asarray(ki), q, k, vt)


def _merge_kernel(x_ref, gt_ref, oa_ref, ob_ref, oc_ref, pa_ref, pb_ref, pc_ref, wo_ref, g1_ref, o_ref):
    D = D_MODEL
    gt = gt_ref[...]
    y = gt[:, 0:D] * _dot(oa_ref[...], pa_ref[...])
    y = y + gt[:, D:2 * D] * _dot(ob_ref[...], pb_ref[...])
    y = y + gt[:, 2 * D:3 * D] * _dot(oc_ref[...], pc_ref[...])
    o_ref[...] = x_ref[...] + g1_ref[0] * _dot(y.astype(BF16), wo_ref[...])


def _merge(x, u_gate, o_a, o_b, o_c, pa, pb, pc, wo, g1, S, tm):
    T, D = x.shape
    tm = min(tm, S)
    rows = lambda w: pl.BlockSpec((tm, w), lambda i: (i, 0))
    full = lambda shape: pl.BlockSpec(shape, lambda i: (0,) * len(shape))
    return pl.pallas_call(
        _merge_kernel,
        out_shape=jax.ShapeDtypeStruct((T, D), F32),
        grid=(T // tm,),
        in_specs=[
            rows(D), rows(GATE_COLS), rows(RW_WIDTH), rows(HG_WIDTH), rows(MLA_HEADS * MLA_PAD),
            full(pa.shape), full(pb.shape), full(pc.shape), full(wo.shape),
            pl.BlockSpec((1, 1, D), lambda i: ((i * tm) // S, 0, 0)),
        ],
        out_specs=rows(D),
        compiler_params=_cparams(("arbitrary",)),
        name="merge",
    )(x, u_gate, o_a, o_b, o_c, pa, pb, pc, wo, g1)


def _ffn_up_kernel(x_ref, sh_ref, sc_ref, wv_ref, wg_ref, cwv_ref, cwg_ref, cbv_ref, cbg_ref, o_ref,
                   h_ref, carry_ref, *, blocks_per_seq):
    j = pl.program_id(1)

    @pl.when(j == 0)
    def _():
        h = _rms(x_ref[...]) * (1.0 + sc_ref[0]) + sh_ref[0]
        h_ref[...] = h.astype(BF16)

    @pl.when(pl.program_id(0) % blocks_per_seq == 0)
    def _():
        carry_ref[j] = jnp.zeros(carry_ref.shape[1:], F32)

    h = h_ref[...]
    tm = h.shape[0]
    carry = carry_ref[j]

    def conv(u, cw_ref, cb_ref, c2, c1):
        row = lax.broadcasted_iota(jnp.int32, (8, u.shape[1]), 0)
        r1 = pltpu.roll(u, 1, 0)
        r2 = pltpu.roll(u, 2, 0)
        u1 = jnp.concatenate([jnp.where(row == 0, c1, r1[0:8, :]), r1[8:, :]], axis=0)
        u2 = jnp.concatenate([jnp.where(row == 0, c2, jnp.where(row == 1, c1, r2[0:8, :])), r2[8:, :]], axis=0)
        cw = cw_ref[...]
        return cw[0:1, :] * u2 + cw[1:2, :] * u1 + cw[2:3, :] * u + cb_ref[...]

    uv = _dot(h, wv_ref[...])
    ug = _dot(h, wg_ref[...])
    val = conv(uv, cwv_ref, cbv_ref, carry[0:1, :], carry[1:2, :])
    gate = conv(ug, cwg_ref, cbg_ref, carry[2:3, :], carry[3:4, :])
    carry_ref[j] = jnp.concatenate([uv[tm - 2:tm, :], ug[tm - 2:tm, :]], axis=0)
    o_ref[...] = (gate * jax.nn.sigmoid(gate) * val).astype(o_ref.dtype)


def _ffn_up(x, shift, scale, wv, wg, cwv, cwg, cbv, cbg, S, tm, tn):
    T, D = x.shape
    tm = min(tm, S)
    ncol = D_FF // tn
    return pl.pallas_call(
        functools.partial(_ffn_up_kernel, blocks_per_seq=S // tm),
        out_shape=jax.ShapeDtypeStruct((T, D_FF), BF16),
        grid=(T // tm, ncol),
        in_specs=[
            pl.BlockSpec((tm, D), lambda i, j: (i, 0)),
            pl.BlockSpec((1, 1, D), lambda i, j: ((i * tm) // S, 0, 0)),
            pl.BlockSpec((1, 1, D), lambda i, j: ((i * tm) // S, 0, 0)),
            pl.BlockSpec((D, tn), lambda i, j: (0, j)),
            pl.BlockSpec((D, tn), lambda i, j: (0, j)),
            pl.BlockSpec((3, tn), lambda i, j: (0, j)),
            pl.BlockSpec((3, tn), lambda i, j: (0, j)),
            pl.BlockSpec((1, tn), lambda i, j: (0, j)),
            pl.BlockSpec((1, tn), lambda i, j: (0, j)),
        ],
        out_specs=pl.BlockSpec((tm, tn), lambda i, j: (i, j)),
        scratch_shapes=[pltpu.VMEM((tm, D), BF16), pltpu.VMEM((ncol, 4, tn), F32)],
        compiler_params=_cparams(("arbitrary", "arbitrary")),
        name="ffn_up",
    )(x, shift, scale, wv, wg, cwv, cwg, cbv, cbg)


def _ffn_down_kernel(x_ref, a_ref, w_ref, g2_ref, fw_ref, o_ref, *, final):
    y = x_ref[...] + g2_ref[0] * _dot(a_ref[...], w_ref[...])
    if final:
        y = _rms(y) * fw_ref[...]
    o_ref[...] = y


def _ffn_down(x, act, w_down, g2, final_w, S, tm, final):
    T, D = x.shape
    tm = min(tm, S)
    return pl.pallas_call(
        functools.partial(_ffn_down_kernel, final=final),
        out_shape=jax.ShapeDtypeStruct((T, D), F32),
        grid=(T // tm,),
        in_specs=[
            pl.BlockSpec((tm, D), lambda i: (i, 0)),
            pl.BlockSpec((tm, D_FF), lambda i: (i, 0)),
            pl.BlockSpec((D_FF, D), lambda i: (0, 0)),
            pl.BlockSpec((1, 1, D), lambda i: ((i * tm) // S, 0, 0)),
            pl.BlockSpec((1, D), lambda i: (0, 0)),
        ],
        out_specs=pl.BlockSpec((tm, D), lambda i: (i, 0)),
        compiler_params=_cparams(("arbitrary",)),
        name="ffn_down",
    )(x, act, w_down, g2, final_w)


def _tiles(S):
    return dict(inproj=min(512, S), gate=min(1024, S), rwkv_prep=min(512, S), hgrn=min(256, S), mla_prep=min(512, S),
                attn=min(2048, S), merge=min(512, S), ffn_up=min(256, S), ffn=min(1024, S))


def _head_pad(w, per_head, lo, hi, dst, pad=MLA_PAD):
    K = w.shape[0]
    w = w.reshape(K, MLA_HEADS, per_head)[:, :, lo:hi]
    out = jnp.zeros((K, MLA_HEADS, pad), w.dtype)
    out = out.at[:, :, dst:dst + (hi - lo)].set(w)
    return out.reshape(K, MLA_HEADS * pad)


def _swap_halves(w):
    half = w.shape[-1] // 2
    return jnp.concatenate([w[..., half:], w[..., :half]], axis=-1)


def kernel(x, c, positions, ada_w, ada_b, w_in, rwkv_mu, rwkv_w0, rwkv_w2, rwkv_a0, rwkv_a2, rwkv_g2, rwkv_k_k, rwkv_k_a, rwkv_r_k, rwkv_ln_w, rwkv_ln_b, hgrn_lb, hgrn_norm_w, mla_q_norm_w, mla_w_uq, mla_kv_norm_w, mla_w_ukv, branch_proj_a, branch_proj_b, branch_proj_c, w_out, ffn_w_up, ffn_conv_w, ffn_conv_b, ffn_w_down, final_norm_w):
    B, S, D = x.shape
    L = ada_w.shape[0]
    T = B * S
    xt = x.reshape(T, D)
    pos = positions.reshape(T, 1)

    mod = _adaln(c, ada_w, ada_b).reshape(L, B, 6, 1, D)

    inv_freq = ROPE_THETA ** (-jnp.arange(0, MLA_ROPE, 2, dtype=F32) / MLA_ROPE)
    invf = jnp.zeros((1, MLA_PAD), F32).at[0, MLA_NOPE:MLA_NOPE + MLA_ROPE].set(jnp.concatenate([inv_freq, inv_freq]))

    ii = np.arange(RW_WIDTH)
    bd = jnp.asarray((ii[:, None] // RW_N == ii[None, :] // RW_N).astype(np.float32))
    cc = np.arange(RW_CHUNK)
    tri_rw = jnp.asarray((cc[:, None] >= cc[None, :]).astype(np.float32)).astype(BF16)
    bd_b = bd.astype(BF16)
    tl = _tiles(S)
    rt_hg = tl['hgrn']
    rr = np.arange(rt_hg)
    tri_hg = jnp.asarray(((rr[:, None] >= rr[None, :])
                          & (rr[:, None] // HG_SUB == rr[None, :] // HG_SUB)).astype(np.float32)).astype(BF16)

    o0 = RW_COLS
    o1 = o0 + HG_COLS
    o2 = o1 + MLA_Q_RANK + MLA_KV_RANK + MLA_ROPE
    for l in range(L):
        sh1, sc1, g1, sh2, sc2, g2 = (mod[l, :, j] for j in range(6))
        wl = w_in[l]
        w_gate = wl[:, o2:o2 + GATE_COLS].astype(BF16)
        w_kr = wl[:, o2 - MLA_ROPE:o2]
        lanes = jnp.zeros((D, MLA_PAD), F32)
        w_u = jnp.concatenate([
            wl[:, o0:o1],
            wl[:, 0:o0],
            wl[:, o1:o1 + MLA_Q_RANK + MLA_KV_RANK],
            lanes.at[:, MLA_NOPE:MLA_NOPE + MLA_ROPE].set(w_kr),
            lanes.at[:, MLA_NOPE:MLA_NOPE + MLA_ROPE].set(_swap_halves(w_kr)),
            jnp.zeros((D, U_COLS - U_MLA - MLA_SEG), F32),
        ], axis=1).astype(BF16)

        u = _modmm(xt, sh1, sc1, w_u, S, tl['inproj'], U_COLS, "inproj")
        gates = _modmm(xt, sh1, sc1, w_gate, S, tl['gate'], GATE_COLS, "inproj_gate", gate=True)

        zpad = jnp.zeros((64, RW_WIDTH), F32)
        r, k2, v, lw, kn, bn, g = _rwkv_prep(
            u, rwkv_mu[l].reshape(1, -1), rwkv_w0[l].reshape(1, -1),
            jnp.concatenate([rwkv_w2[l], zpad], axis=0).astype(BF16), rwkv_a0[l].reshape(1, -1),
            jnp.concatenate([zpad, rwkv_a2[l]], axis=0).astype(BF16), rwkv_g2[l].astype(BF16),
            rwkv_k_k[l].reshape(1, -1), rwkv_k_a[l].reshape(1, -1), bd_b, S, tl['rwkv_prep'])
        o_a = _rwkv_scan(r, k2, v, lw, kn, bn, g, rwkv_ln_w[l].reshape(1, -1), rwkv_ln_b[l].reshape(1, -1),
                         rwkv_r_k[l].reshape(1, -1), tri_rw, bd_b, B, S)

        o_b = _hgrn(u, hgrn_lb, hgrn_norm_w[l].reshape(1, -1), tri_hg, l, B, S, rt_hg)

        per_q = MLA_NOPE + MLA_ROPE
        wq = _head_pad(mla_w_uq[l], per_q, 0, per_q, 0).astype(BF16)
        wq_rope = mla_w_uq[l].reshape(MLA_Q_RANK, MLA_HEADS, per_q)[:, :, MLA_NOPE:]
        wqs = _head_pad(_swap_halves(wq_rope).reshape(MLA_Q_RANK, -1), MLA_ROPE, 0, MLA_ROPE, MLA_NOPE).astype(BF16)
        per_kv = MLA_NOPE + MLA_V
        wk = _head_pad(mla_w_ukv[l], per_kv, 0, MLA_NOPE, 0).astype(BF16)
        wvt = _head_pad(mla_w_ukv[l], per_kv, MLA_NOPE, per_kv, 0).T.astype(BF16)
        q_cat, k_cat, v_t = _mla_prep(u, pos, invf, mla_q_norm_w[l].reshape(1, -1),
                                      mla_kv_norm_w[l].reshape(1, -1), wq, wqs, wk, wvt, tl['mla_prep'])
        o_c = _mla_attn(q_cat, k_cat, v_t, B, S, tl['attn'])

        pc = jnp.zeros((MLA_HEADS, MLA_PAD, D), F32).at[:, :MLA_V, :].set(
            branch_proj_c[l].reshape(MLA_HEADS, MLA_V, D)).reshape(MLA_HEADS * MLA_PAD, D).astype(BF16)
        xt = _merge(xt, gates, o_a, o_b, o_c, branch_proj_a[l].astype(BF16), branch_proj_b[l].astype(BF16), pc,
                    w_out[l].astype(BF16), g1, S, tl['merge'])

        wu = ffn_w_up[l]
        cw = ffn_conv_w[l]
        cb = ffn_conv_b[l].reshape(1, -1)
        act = _ffn_up(xt, sh2, sc2, wu[:, :D_FF].astype(BF16), wu[:, D_FF:].astype(BF16),
                      cw[:, :D_FF], cw[:, D_FF:], cb[:, :D_FF], cb[:, D_FF:], S, tl['ffn_up'], D_FF)
        xt = _ffn_down(xt, act, ffn_w_down[l].astype(BF16), g2, final_norm_w.reshape(1, -1), S, tl['ffn'], l == L - 1)

    return xt.reshape(B, S, D)
```
